```python
import jax, jax.numpy as jnp
from jax import lax
import numpy as np

D_MODEL = 1024
BATCH = 8
SEQ = 2048
DEPTH = 1
DEC_BATCH = 128
DEC_SEQ = 1
PAST_LEN = 16384
PAGE_SIZE = 128

N_META = 16
MIX_WIDTH = D_MODEL
CONV_WIDTH = MIX_WIDTH // 2
CONV_K = 3
RWKV_WIDTH = MIX_WIDTH - CONV_WIDTH
RWKV_HEAD = 64
RWKV_HEADS = RWKV_WIDTH // RWKV_HEAD
W_LORA = 32
A_LORA = 32
G_LORA = 96
RW_COLS = 3 * RWKV_WIDTH + W_LORA + A_LORA + G_LORA
IN_COLS = 3 * CONV_WIDTH + RW_COLS
PEER_HEADS = 8
N_KEYS = 128
N_EXPERTS = N_KEYS * N_KEYS
PEER_TOPK = 16
QUERY_DIM = 256
HALF = QUERY_DIM // 2
PEER_BLOCK = 128
LN_EPS = 1e-5
GN_EPS = RWKV_HEAD * 1e-5
ALPHA = (2 * DEPTH) ** 0.25
BETA = (8 * DEPTH) ** -0.25

kernel_name = 'hymba_conv_rwkv7_peer_step'


def layer_norm(x, g, b, eps=LN_EPS):
    xf = x.astype(jnp.float32)
    m = jnp.mean(xf, axis=-1, keepdims=True)
    var = jnp.mean(jnp.square(xf - m), axis=-1, keepdims=True)
    return ((xf - m) * lax.rsqrt(var + eps) * g + b).astype(x.dtype)


def wkv_scan(state, r, w, k, v, kk, a):
    def step(s, inp):
        r_t, w_t, k_t, v_t, kk_t, a_t = inp
        sa = jnp.einsum('bhvk,bhk->bhv', s, -kk_t)
        s = (s * w_t[:, :, None, :] + sa[..., None] * (kk_t * a_t)[:, :, None, :]
             + v_t[..., None] * k_t[:, :, None, :])
        return s, jnp.einsum('bhvk,bhk->bhv', s, r_t)
    xs = tuple(jnp.moveaxis(t, 1, 0) for t in (r, w, k, v, kk, a))
    s_fin, o = lax.scan(step, state, xs)
    return jnp.moveaxis(o, 0, 1), s_fin


def short_conv_group(p_cv, conv_buf, conv_w):
    S = p_cv.shape[1]
    bg, cg, hv = jnp.split(p_cv, 3, axis=-1)
    z = cg * hv
    zp = jnp.concatenate([conv_buf.astype(z.dtype), z], axis=1)
    conv = sum(conv_w[j] * zp[:, j:j + S] for j in range(CONV_K))
    return bg * conv, zp[:, -(CONV_K - 1):]


def rwkv7_group(p_rw, prev_rw, state, mu, w0, w_up, a0, a_up, g_up, k_k, k_a, r_k, gn_g, gn_b):
    B, S, _ = p_rw.shape
    f32 = jnp.float32
    shifted = jnp.concatenate([prev_rw[:, None, :], p_rw[:, :-1]], axis=1)
    m = p_rw + (shifted - p_rw) * mu
    cuts = [RWKV_WIDTH, 2 * RWKV_WIDTH, 3 * RWKV_WIDTH, 3 * RWKV_WIDTH + W_LORA,
            3 * RWKV_WIDTH + W_LORA + A_LORA]
    r, k, v, wd, ad, gd = jnp.split(m, cuts, axis=-1)
    w_log = -jax.nn.softplus(-(w0 + jnp.tanh(wd) @ w_up).astype(f32)) - 0.5
    decay = jnp.exp(-jnp.exp(w_log))
    a = jax.nn.sigmoid((a0 + ad @ a_up).astype(f32))
    g = jax.nn.sigmoid(gd) @ g_up
    heads = lambda t: t.astype(f32).reshape(B, S, RWKV_HEADS, RWKV_HEAD)
    r, k, v, decay, a = heads(r), heads(k), heads(v), heads(decay), heads(a)
    kk = k * k_k.astype(f32).reshape(RWKV_HEADS, RWKV_HEAD)
    kk = kk / jnp.maximum(jnp.sqrt(jnp.sum(kk * kk, axis=-1, keepdims=True)), 1e-12)
    k = k * (1.0 + (a - 1.0) * k_a.astype(f32).reshape(RWKV_HEADS, RWKV_HEAD))
    o, s_new = wkv_scan(state.astype(f32), r, decay, k, v, kk, a)
    om = jnp.mean(o, axis=-1, keepdims=True)
    ov = jnp.mean(jnp.square(o - om), axis=-1, keepdims=True)
    o = ((o - om) * lax.rsqrt(ov + GN_EPS) * gn_g.reshape(RWKV_HEADS, RWKV_HEAD)
         + gn_b.reshape(RWKV_HEADS, RWKV_HEAD))
    o = o + jnp.sum(r * k * r_k.astype(f32), axis=-1, keepdims=True) * v
    y = o.reshape(B, S, RWKV_WIDTH).astype(p_rw.dtype) * g
    return y, s_new


def peer_ffn(x, w_q, sub_keys, peer_u, peer_v):
    B, S, D = x.shape
    T = B * S
    xt = x.reshape(T, D)
    q = (xt @ w_q).astype(jnp.float32).reshape(T, PEER_HEADS, 2, HALF)
    sc = jnp.einsum('thpd,hpnd->thpn', q, sub_keys.astype(jnp.float32))
    s_top, i_top = lax.top_k(sc, PEER_TOPK)
    cand = (s_top[:, :, 0, :, None] + s_top[:, :, 1, None, :]).reshape(T, PEER_HEADS, PEER_TOPK * PEER_TOPK)
    cidx = (i_top[:, :, 0, :, None] * N_KEYS + i_top[:, :, 1, None, :]).reshape(T, PEER_HEADS, PEER_TOPK * PEER_TOPK)
    best, pos = lax.top_k(cand, PEER_TOPK)
    idx = jnp.take_along_axis(cidx, pos, axis=-1).reshape(T, PEER_HEADS * PEER_TOPK)
    gate = jax.nn.softmax(best, axis=-1).reshape(T, PEER_HEADS * PEER_TOPK).astype(x.dtype)
    n_blk = -(-T // PEER_BLOCK)
    pad = n_blk * PEER_BLOCK - T
    xb = jnp.pad(xt, ((0, pad), (0, 0))).reshape(n_blk, PEER_BLOCK, D)
    ib = jnp.pad(idx, ((0, pad), (0, 0))).reshape(n_blk, PEER_BLOCK, -1)
    gb = jnp.pad(gate, ((0, pad), (0, 0))).reshape(n_blk, PEER_BLOCK, -1)

    def block(args):
        xc, ic, gc = args
        hu = jnp.einsum('tkd,td->tk', peer_u[ic], xc)
        act = jax.nn.gelu(hu, approximate=False) * gc
        return jnp.einsum('tk,tkd->td', act, peer_v[ic])

    out = lax.map(block, (xb, ib, gb)).reshape(n_blk * PEER_BLOCK, D)[:T]
    return out.reshape(B, S, D)


def decoder_layer(h, shift_prev, conv_buf, wkv, lp):
    (w_in, conv_w, mu, w0, w_up, a0, a_up, g_up, k_k, k_a, r_k, gn_g, gn_b, w_o,
     ln1_g, ln1_b, w_q, sub_keys, peer_u, peer_v, ln2_g, ln2_b) = lp
    proj = h @ w_in
    prev_rw = shift_prev.astype(h.dtype) @ w_in[:, 3 * CONV_WIDTH:]
    y_cv, conv_new = short_conv_group(proj[..., :3 * CONV_WIDTH], conv_buf, conv_w)
    y_rw, wkv_new = rwkv7_group(proj[..., 3 * CONV_WIDTH:], prev_rw, wkv, mu, w0, w_up, a0, a_up,
                                g_up, k_k, k_a, r_k, gn_g, gn_b)
    mix = jnp.concatenate([y_cv, y_rw], axis=-1) @ w_o
    h1 = layer_norm(ALPHA * h + mix, ln1_g, ln1_b)
    h2 = layer_norm(ALPHA * h1 + peer_ffn(h1, w_q, sub_keys, peer_u, peer_v), ln2_g, ln2_b)
    return h2, h[:, -1], conv_new, wkv_new


def trunk(h, shift0, conv0, wkv0, params):
    shifts, convs, wkvs = [], [], []
    for l in range(DEPTH):
        lp = tuple(p[l] for p in params)
        h, s_new, c_new, w_new = decoder_layer(h, shift0[l], conv0[l], wkv0[l], lp)
        shifts.append(s_new.astype(h.dtype))
        convs.append(c_new.astype(h.dtype))
        wkvs.append(w_new.astype(h.dtype))
    return h, jnp.stack(convs), jnp.stack(shifts), jnp.stack(wkvs)


def setup_inputs(seed: int = 0) -> dict:
    key = jax.random.key(seed)
    ks = jax.random.split(key, 32)
    f32 = jnp.float32
    nrm = lambda k, shape, s: jax.random.normal(k, shape, f32) * s
    L = DEPTH
    return {
        'x_prompt': nrm(ks[0], (BATCH, SEQ, D_MODEL), 1.0),
        'x_sample': nrm(ks[1], (DEC_BATCH, DEC_SEQ, D_MODEL), 1.0),
        'state_conv': nrm(ks[2], (L, DEC_BATCH, CONV_K - 1, CONV_WIDTH), 0.5),
        'state_shift': nrm(ks[3], (L, DEC_BATCH, D_MODEL), 1.0),
        'state_wkv': nrm(ks[4], (L, DEC_BATCH, RWKV_HEADS, RWKV_HEAD, RWKV_HEAD), 0.3),
        'meta_tokens': nrm(ks[5], (N_META, D_MODEL), 1.0),
        'ln_in_g': 1.0 + nrm(ks[6], (D_MODEL,), 0.02),
        'ln_in_b': nrm(ks[7], (D_MODEL,), 0.02),
        'w_in': nrm(ks[8], (L, D_MODEL, IN_COLS), D_MODEL ** -0.5),
        'conv_w': nrm(ks[9], (L, CONV_K, CONV_WIDTH), CONV_K ** -0.5),
        'mu': jax.random.uniform(ks[10], (L, RW_COLS), f32),
        'w0': jax.random.uniform(ks[11], (L, RWKV_WIDTH), f32, -3.0, 2.0),
        'w_up': nrm(ks[12], (L, W_LORA, RWKV_WIDTH), 0.5 * W_LORA ** -0.5),
        'a0': nrm(ks[13], (L, RWKV_WIDTH), 0.5),
        'a_up': nrm(ks[14], (L, A_LORA, RWKV_WIDTH), A_LORA ** -0.5),
        'g_up': nrm(ks[15], (L, G_LORA, RWKV_WIDTH), G_LORA ** -0.5),
        'k_k': 0.85 + nrm(ks[16], (L, RWKV_WIDTH), 0.05),
        'k_a': 1.0 + nrm(ks[17], (L, RWKV_WIDTH), 0.05),
        'r_k': nrm(ks[18], (L, RWKV_HEADS, RWKV_HEAD), 0.1),
        'gn_g': 1.0 + nrm(ks[19], (L, RWKV_WIDTH), 0.02),
        'gn_b': nrm(ks[20], (L, RWKV_WIDTH), 0.02),
        'w_o': nrm(ks[21], (L, MIX_WIDTH, D_MODEL), BETA * MIX_WIDTH ** -0.5),
        'ln1_g': 1.0 + nrm(ks[22], (L, D_MODEL), 0.02),
        'ln1_b': nrm(ks[23], (L, D_MODEL), 0.02),
        'w_q': nrm(ks[24], (L, D_MODEL, PEER_HEADS * QUERY_DIM), D_MODEL ** -0.5),
        'sub_keys': nrm(ks[25], (L, PEER_HEADS, 2, N_KEYS, HALF), HALF ** -0.5),
        'peer_u': nrm(ks[26], (L, N_EXPERTS, D_MODEL), D_MODEL ** -0.5),
        'peer_v': nrm(ks[27], (L, N_EXPERTS, D_MODEL), BETA),
        'ln2_g': 1.0 + nrm(ks[28], (L, D_MODEL), 0.02),
        'ln2_b': nrm(ks[29], (L, D_MODEL), 0.02),
    }


def reference(x_prompt, x_sample, state_conv, state_shift, state_wkv, meta_tokens, ln_in_g, ln_in_b,
              w_in, conv_w, mu, w0, w_up, a0, a_up, g_up, k_k, k_a, r_k, gn_g, gn_b, w_o,
              ln1_g, ln1_b, w_q, sub_keys, peer_u, peer_v, ln2_g, ln2_b):
    params = (w_in, conv_w, mu, w0, w_up, a0, a_up, g_up, k_k, k_a, r_k, gn_g, gn_b, w_o,
              ln1_g, ln1_b, w_q, sub_keys, peer_u, peer_v, ln2_g, ln2_b)
    dt = x_prompt.dtype
    meta = jnp.broadcast_to(meta_tokens.astype(dt)[None], (BATCH, N_META, D_MODEL))
    hp = layer_norm(jnp.concatenate([meta, x_prompt], axis=1), ln_in_g, ln_in_b)
    z_shift = jnp.zeros((DEPTH, BATCH, D_MODEL), dt)
    z_conv = jnp.zeros((DEPTH, BATCH, CONV_K - 1, CONV_WIDTH), dt)
    z_wkv = jnp.zeros((DEPTH, BATCH, RWKV_HEADS, RWKV_HEAD, RWKV_HEAD), jnp.float32)
    hp, p_conv, p_shift, p_wkv = trunk(hp, z_shift, z_conv, z_wkv, params)
    y_prompt = hp[:, N_META:]
    hs = layer_norm(x_sample, ln_in_g, ln_in_b)
    y_sample, s_conv, s_shift, s_wkv = trunk(hs, state_shift, state_conv, state_wkv, params)
    return (y_prompt, y_sample, p_conv, p_shift, p_wkv, s_conv, s_shift, s_wkv)
```

```python
import functools

import numpy as np
import jax
import jax.numpy as jnp
from jax import lax
from jax.experimental import pallas as pl
from jax.experimental.pallas import tpu as pltpu

F32 = jnp.float32
BF16 = jnp.bfloat16

D_MODEL = 1024
N_META = 16
BATCH = 8
SEQ = 2048
S_P = SEQ + N_META
T_P = BATCH * S_P
DEC_BATCH = 128
T_ALL = T_P + DEC_BATCH
CONV_WIDTH = 512
RWKV_WIDTH = 512
HEAD = 64
HEADS = 8
W_LORA = 32
A_LORA = 32
G_LORA = 96
CV_COLS = 3 * CONV_WIDTH
LORA_SLOT = 128
RW_PAD = 3 * RWKV_WIDTH + 3 * LORA_SLOT
PEER_HEADS = 8
N_KEYS = 128
N_EXPERTS = N_KEYS * N_KEYS
TOPK = 16
HALF = 128
LN_EPS = 1e-5
GN_EPS = HEAD * 1e-5
ALPHA = 2.0 ** 0.25
SQRT_HALF = float(np.sqrt(0.5))
NOT_RANKED = 1e9
VMEM_LIMIT_BYTES = 56 * 1024 * 1024

TB_PROJ = 256
TB_MIX = 344
TB_OUT = 384
TS_SCAN = 48
TB_ROUTE = 256
TB_PEER = 640
I_PER_CHUNK = 8
E_CHUNK = I_PER_CHUNK * N_KEYS


def _params(sem):
    return pltpu.CompilerParams(dimension_semantics=sem, vmem_limit_bytes=VMEM_LIMIT_BYTES)


def _layer_norm(x, g, b):
    m = jnp.mean(x, axis=-1, keepdims=True)
    xc = x - m
    var = jnp.mean(xc * xc, axis=-1, keepdims=True)
    return xc * lax.rsqrt(var + LN_EPS) * g + b


def _headsum(x, ones_bd):
    hi = x.astype(BF16)
    lo = (x - hi.astype(F32)).astype(BF16)
    return (jnp.dot(hi, ones_bd, preferred_element_type=F32)
            + jnp.dot(lo, ones_bd, preferred_element_type=F32))


def _ln_proj_kernel(x_ref, g_ref, b_ref, wcv_ref, wrw_ref, h_ref, pcv_ref, prw_ref):
    h = _layer_norm(x_ref[...], g_ref[...], b_ref[...])
    h_ref[...] = h
    hb = h.astype(BF16)
    pcv_ref[...] = jnp.dot(hb, wcv_ref[...], preferred_element_type=F32)
    prw_ref[...] = jnp.dot(hb, wrw_ref[...], preferred_element_type=F32)


def _ln_proj(x_all, g, b, wcv, wrw):
    n = x_all.shape[0]
    row = lambda i: (i, 0)
    fixed = lambda i: (0, 0)
    return pl.pallas_call(
        _ln_proj_kernel,
        grid=(n // TB_PROJ,),
        in_specs=[pl.BlockSpec((TB_PROJ, D_MODEL), row),
                  pl.BlockSpec((1, D_MODEL), fixed),
                  pl.BlockSpec((1, D_MODEL), fixed),
                  pl.BlockSpec((D_MODEL, CV_COLS), fixed),
                  pl.BlockSpec((D_MODEL, RW_PAD), fixed)],
        out_specs=[pl.BlockSpec((TB_PROJ, D_MODEL), row),
                   pl.BlockSpec((TB_PROJ, CV_COLS), row),
                   pl.BlockSpec((TB_PROJ, RW_PAD), row)],
        out_shape=[jax.ShapeDtypeStruct((n, D_MODEL), F32),
                   jax.ShapeDtypeStruct((n, CV_COLS), F32),
                   jax.ShapeDtypeStruct((n, RW_PAD), F32)],
        compiler_params=_params(("parallel",)),
        name="ln_proj",
    )(x_all, g, b, wcv, wrw)


def _shift_proj_kernel(x_ref, wrw_ref, prw_ref):
    prw_ref[...] = jnp.dot(x_ref[...].astype(BF16), wrw_ref[...], preferred_element_type=F32)


def _shift_proj(x, wrw):
    n = x.shape[0]
    return pl.pallas_call(
        _shift_proj_kernel,
        out_shape=jax.ShapeDtypeStruct((n, RW_PAD), F32),
        compiler_params=_params(None),
        name="shift_proj",
    )(x, wrw)


N_MIX_OUT = 9


def _mixer_core(pcv, prw, prw_prev, z1, z2, cw, mu, w0, wup, a0, aup, gup, kkw, kaw, rkw, ones_bd):
    bg = pcv[:, 0:CONV_WIDTH]
    z = pcv[:, CONV_WIDTH:2 * CONV_WIDTH] * pcv[:, 2 * CONV_WIDTH:3 * CONV_WIDTH]
    conv = cw[0:1, :] * z2 + cw[1:2, :] * z1 + cw[2:3, :] * z
    y_cv = bg * conv

    m = prw + (prw_prev - prw) * mu
    r = m[:, 0:RWKV_WIDTH]
    k = m[:, RWKV_WIDTH:2 * RWKV_WIDTH]
    v = m[:, 2 * RWKV_WIDTH:3 * RWKV_WIDTH]
    base = 3 * RWKV_WIDTH
    wd = m[:, base:base + LORA_SLOT]
    ad = m[:, base + LORA_SLOT:base + 2 * LORA_SLOT]
    gd = m[:, base + 2 * LORA_SLOT:base + 3 * LORA_SLOT]

    xw = -(w0 + jnp.dot(jnp.tanh(wd).astype(BF16), wup, preferred_element_type=F32))
    softplus = jnp.maximum(xw, 0.0) + jnp.log1p(jnp.exp(-jnp.abs(xw)))
    w_log = -softplus - 0.5
    decay = jnp.exp(-jnp.exp(w_log))
    a = jax.nn.sigmoid(a0 + jnp.dot(ad.astype(BF16), aup, preferred_element_type=F32))
    g = jnp.dot(jax.nn.sigmoid(gd).astype(BF16), gup, preferred_element_type=F32)

    kk = k * kkw
    norm = jnp.sqrt(_headsum(kk * kk, ones_bd))
    kk = kk / jnp.maximum(norm, 1e-12)
    kx = k * (1.0 + (a - 1.0) * kaw)
    bonus_v = _headsum(r * kx * rkw, ones_bd) * v
    return y_cv, z, (y_cv, r, decay, kx, v, -kk, kk * a, g, bonus_v)


def _mixer_seq_kernel(pcv_ref, prw_ref, conv0_ref, prw0_ref, cw_ref, mu_ref, w0_ref, wup_ref, a0_ref, aup_ref,
                      gup_ref, kkw_ref, kaw_ref, rkw_ref, ones_ref, *rest):
    out_refs = rest[:N_MIX_OUT]
    convt_ref = rest[N_MIX_OUT]
    zc_ref, pc_ref = rest[N_MIX_OUT + 1:]
    tb = pl.program_id(1)

    @pl.when(tb == 0)
    def _():
        zc_ref[...] = conv0_ref[0]
        pc_ref[...] = prw0_ref[0]

    pcv = pcv_ref[...]
    prw = prw_ref[...]
    n = pcv.shape[0]
    z = pcv[:, CONV_WIDTH:2 * CONV_WIDTH] * pcv[:, 2 * CONV_WIDTH:3 * CONV_WIDTH]
    row_z = lax.broadcasted_iota(jnp.int32, z.shape, 0)
    row_p = lax.broadcasted_iota(jnp.int32, prw.shape, 0)
    c2 = zc_ref[6:7, :]
    c1 = zc_ref[7:8, :]
    z1 = jnp.where(row_z == 0, c1, pltpu.roll(z, 1, axis=0))
    z2 = jnp.where(row_z == 0, c2, jnp.where(row_z == 1, c1, pltpu.roll(z, 2, axis=0)))
    prw_prev = jnp.where(row_p == 0, pc_ref[7:8, :], pltpu.roll(prw, 1, axis=0))

    _, _, outs = _mixer_core(pcv, prw, prw_prev, z1, z2, cw_ref[...], mu_ref[...], w0_ref[...], wup_ref[...],
                             a0_ref[...], aup_ref[...], gup_ref[...], kkw_ref[...], kaw_ref[...], rkw_ref[...],
                             ones_ref[...])
    for ref, val in zip(out_refs, outs):
        ref[...] = val
    tail = z[n - 8:n, :]
    zc_ref[...] = tail
    pc_ref[...] = prw[n - 8:n, :]
    convt_ref[0] = tail


def _mixer_row_kernel(pcv_ref, prw_ref, prwprev_ref, z1_ref, z2_ref, cw_ref, mu_ref, w0_ref, wup_ref, a0_ref,
                      aup_ref, gup_ref, kkw_ref, kaw_ref, rkw_ref, ones_ref, *rest):
    out_refs = rest[:N_MIX_OUT]
    z_ref = rest[N_MIX_OUT]
    _, z, outs = _mixer_core(pcv_ref[...], prw_ref[...], prwprev_ref[...], z1_ref[...], z2_ref[...], cw_ref[...],
                             mu_ref[...], w0_ref[...], wup_ref[...], a0_ref[...], aup_ref[...], gup_ref[...],
                             kkw_ref[...], kaw_ref[...], rkw_ref[...], ones_ref[...])
    for ref, val in zip(out_refs, outs):
        ref[...] = val
    z_ref[...] = z


def _mixer_param_specs(nargs_grid):
    fixed = (lambda *_: (0, 0))
    shapes = [(3, CONV_WIDTH), (1, RW_PAD), (1, RWKV_WIDTH), (LORA_SLOT, RWKV_WIDTH), (1, RWKV_WIDTH),
              (LORA_SLOT, RWKV_WIDTH), (LORA_SLOT, RWKV_WIDTH), (1, RWKV_WIDTH), (1, RWKV_WIDTH), (1, RWKV_WIDTH),
              (RWKV_WIDTH, RWKV_WIDTH)]
    return [pl.BlockSpec(s, fixed) for s in shapes]


def _mixer_seq(pcv, prw, conv0, prw0, mix_params):
    nb = S_P // TB_MIX
    row = lambda b, t: (b * nb + t, 0)
    seq = lambda b, t: (b, 0, 0)
    out512 = pl.BlockSpec((TB_MIX, RWKV_WIDTH), row)
    return pl.pallas_call(
        _mixer_seq_kernel,
        grid=(BATCH, nb),
        in_specs=[pl.BlockSpec((TB_MIX, CV_COLS), row),
                  pl.BlockSpec((TB_MIX, RW_PAD), row),
                  pl.BlockSpec((1, 8, CONV_WIDTH), seq),
                  pl.BlockSpec((1, 8, RW_PAD), seq)] + _mixer_param_specs(2),
        out_specs=[out512] * N_MIX_OUT + [pl.BlockSpec((1, 8, CONV_WIDTH), seq)],
        out_shape=[jax.ShapeDtypeStruct((T_P, RWKV_WIDTH), F32)] * N_MIX_OUT
                  + [jax.ShapeDtypeStruct((BATCH, 8, CONV_WIDTH), F32)],
        scratch_shapes=[pltpu.VMEM((8, CONV_WIDTH), F32), pltpu.VMEM((8, RW_PAD), F32)],
        compiler_params=_params(("arbitrary", "arbitrary")),
        name="mixer_seq",
    )(pcv, prw, conv0, prw0, *mix_params)


def _mixer_rows(pcv, prw, prw_prev, z1, z2, mix_params):
    n = pcv.shape[0]
    return pl.pallas_call(
        _mixer_row_kernel,
        out_shape=[jax.ShapeDtypeStruct((n, RWKV_WIDTH), F32)] * (N_MIX_OUT + 1),
        compiler_params=_params(None),
        name="mixer_rows",
    )(pcv, prw, prw_prev, z1, z2, *mix_params)


V_ROWS = HEAD // 2
LANES = 128


def _wkv_kernel(r_ref, w_ref, k_ref, v_ref, nkk_ref, bb_ref, s0_ref, o_ref, st_ref, s_scr, *, steps):
    tc = pl.program_id(1)

    @pl.when(tc == 0)
    def _():
        s_scr[...] = s0_ref[0]

    def step(t, carry):
        r = r_ref[0, t]
        w = w_ref[0, t]
        kx = k_ref[0, t]
        nkk = nkk_ref[0, t]
        bb = bb_ref[0, t]
        for vl in range(V_ROWS):
            sv = s_scr[vl]
            sa = jnp.sum(sv * nkk, axis=0, keepdims=True)
            vrow = v_ref[0, t, pl.ds(vl, 1), :]
            sn = sv * w + sa * bb + vrow * kx
            s_scr[vl] = sn
            o_ref[0, t, pl.ds(vl, 1), :] = jnp.sum(sn * r, axis=0, keepdims=True)
        return carry

    lax.fori_loop(0, steps, step, 0)

    @pl.when(tc == pl.num_programs(1) - 1)
    def _():
        st_ref[0] = s_scr[...]


def _wkv_scan(r, w, k, v, nkk, bb, s0, steps):
    groups, s_len = r.shape[0], r.shape[1]
    key_spec = pl.BlockSpec((1, steps, HEAD, LANES), lambda g, t: (g, t, 0, 0))
    val_spec = pl.BlockSpec((1, steps, V_ROWS, LANES), lambda g, t: (g, t, 0, 0))
    st_spec = pl.BlockSpec((1, V_ROWS, HEAD, LANES), lambda g, t: (g, 0, 0, 0))
    return pl.pallas_call(
        functools.partial(_wkv_kernel, steps=steps),
        grid=(groups, s_len // steps),
        in_specs=[key_spec, key_spec, key_spec, val_spec, key_spec, key_spec, st_spec],
        out_specs=[val_spec, st_spec],
        out_shape=[jax.ShapeDtypeStruct((groups, s_len, V_ROWS, LANES), F32),
                   jax.ShapeDtypeStruct((groups, V_ROWS, HEAD, LANES), F32)],
        scratch_shapes=[pltpu.VMEM((V_ROWS, HEAD, LANES), F32)],
        compiler_params=_params(("arbitrary", "arbitrary")),
        name="wkv_scan",
    )(r, w, k, v, nkk, bb, s0)


def _to_scan_keys(x, nb, s_len):
    g = nb // BATCH
    x = x.reshape(g, BATCH, s_len, HEADS, HEAD).transpose(0, 2, 4, 1, 3).reshape(g, s_len, HEAD, BATCH * HEADS)
    return jnp.concatenate([x, x], axis=-1)


def _to_scan_vals(x, nb, s_len):
    g = nb // BATCH
    x = x.reshape(g, BATCH, s_len, HEADS, 2, V_ROWS).transpose(0, 2, 5, 4, 1, 3)
    return x.reshape(g, s_len, V_ROWS, LANES)


def _from_scan_vals(o, nb, s_len):
    g = nb // BATCH
    o = o.reshape(g, s_len, V_ROWS, 2, BATCH, HEADS).transpose(0, 4, 1, 5, 3, 2)
    return o.reshape(nb * s_len, RWKV_WIDTH)


def _state_to_scan(s, nb):
    g = nb // BATCH
    s = s.reshape(g, BATCH, HEADS, 2, V_ROWS, HEAD).transpose(0, 4, 5, 3, 1, 2)
    return s.reshape(g, V_ROWS, HEAD, LANES)


def _state_from_scan(s, nb):
    g = nb // BATCH
    s = s.reshape(g, V_ROWS, HEAD, 2, BATCH, HEADS).transpose(0, 4, 5, 3, 1, 2)
    return s.reshape(nb, HEADS, HEAD, HEAD)


def _mix_out_kernel(o_ref, g_ref, bv_ref, ycv_ref, h_ref, gng_ref, gnb_ref, ones_ref, wo1_ref, wo2_ref,
                    lg_ref, lb_ref, h1_ref, h1b_ref):
    ones_bd = ones_ref[...]
    o = o_ref[...]
    d = o - _headsum(o, ones_bd) * (1.0 / HEAD)
    var = _headsum(d * d, ones_bd) * (1.0 / HEAD)
    gn = d * lax.rsqrt(var + GN_EPS) * gng_ref[...] + gnb_ref[...]
    y_rw = (gn + bv_ref[...]) * g_ref[...]
    mix = (jnp.dot(ycv_ref[...].astype(BF16), wo1_ref[...], preferred_element_type=F32)
           + jnp.dot(y_rw.astype(BF16), wo2_ref[...], preferred_element_type=F32))
    h1 = _layer_norm(ALPHA * h_ref[...] + mix, lg_ref[...], lb_ref[...])
    h1_ref[...] = h1
    h1b_ref[...] = h1.astype(BF16)


def _mix_out(o, g, bv, ycv, h, h_block0, tb, out_params):
    n = o.shape[0]
    row = lambda i: (i, 0)
    hrow = lambda i: (i + h_block0, 0)
    fixed = lambda i: (0, 0)
    s512 = pl.BlockSpec((tb, RWKV_WIDTH), row)
    return pl.pallas_call(
        _mix_out_kernel,
        grid=(n // tb,),
        in_specs=[s512, s512, s512, s512, pl.BlockSpec((tb, D_MODEL), hrow),
                  pl.BlockSpec((1, RWKV_WIDTH), fixed), pl.BlockSpec((1, RWKV_WIDTH), fixed),
                  pl.BlockSpec((RWKV_WIDTH, RWKV_WIDTH), fixed),
                  pl.BlockSpec((CONV_WIDTH, D_MODEL), fixed), pl.BlockSpec((RWKV_WIDTH, D_MODEL), fixed),
                  pl.BlockSpec((1, D_MODEL), fixed), pl.BlockSpec((1, D_MODEL), fixed)],
        out_specs=[pl.BlockSpec((tb, D_MODEL), row), pl.BlockSpec((tb, D_MODEL), row)],
        out_shape=[jax.ShapeDtypeStruct((n, D_MODEL), F32), jax.ShapeDtypeStruct((n, D_MODEL), BF16)],
        compiler_params=_params(("parallel",)),
        name="mix_out",
    )(o, g, bv, ycv, h, *out_params)


CAND_ROWS = TOPK + 7 * 8 + 8


def _extract_top(s, ids, count):
    rank = jnp.full(s.shape, NOT_RANKED, F32)
    big = jnp.float32(NOT_RANKED)
    vals, picks = [], []
    for p in range(count):
        m = jnp.max(s, axis=0, keepdims=True)
        pick = jnp.min(jnp.where(s == m, ids, big), axis=0, keepdims=True)
        hit = ids == pick
        rank = jnp.where(hit, jnp.float32(p), rank)
        s = jnp.where(hit, -jnp.inf, s)
        vals.append(m)
        picks.append(pick)
    return vals, picks, rank


def _peer_route_kernel(x_ref, wqt_ref, keys_ref, n1_ref, e1_ref, r2_ref, e2_ref):
    xb = x_ref[...]
    qt = lax.dot_general(wqt_ref[...], xb, (((1,), (1,)), ((), ())), preferred_element_type=F32)
    lanes = xb.shape[0]
    key_ids = lax.broadcasted_iota(jnp.int32, (N_KEYS, lanes), 0).astype(F32)
    row16 = lax.broadcasted_iota(jnp.int32, (TOPK, lanes), 0).astype(F32)
    row8 = lax.broadcasted_iota(jnp.int32, (8, lanes), 0).astype(F32)
    cand_ids = jnp.concatenate([row16] + [row8 + float(TOPK * p) for p in range(1, 8)]
                               + [(row8 + 8.0) * float(TOPK)], axis=0)
    for h in range(PEER_HEADS):
        s1 = jnp.dot(keys_ref[2 * h], qt[(2 * h) * HALF:(2 * h + 1) * HALF, :].astype(BF16),
                     preferred_element_type=F32)
        s2 = jnp.dot(keys_ref[2 * h + 1], qt[(2 * h + 1) * HALF:(2 * h + 2) * HALF, :].astype(BF16),
                     preferred_element_type=F32)
        a_vals, _, rank1 = _extract_top(s1, key_ids, TOPK)
        b_vals, _, rank2 = _extract_top(s2, key_ids, TOPK)
        b = jnp.zeros((TOPK, lanes), F32)
        a_hi = jnp.zeros((8, lanes), F32)
        for p in range(TOPK):
            b = jnp.where(row16 == float(p), b_vals[p], b)
        for p in range(8):
            a_hi = jnp.where(row8 == float(p), a_vals[8 + p], a_hi)
        cand = jnp.concatenate([a_vals[0] + b] + [a_vals[p] + b[0:8, :] for p in range(1, 8)]
                               + [a_hi + b_vals[0]], axis=0)
        _, _, crank = _extract_top(cand, cand_ids, TOPK)
        sel = crank < float(TOPK)
        top = a_vals[0] + b_vals[0]
        z = jnp.sum(jnp.where(sel, jnp.exp(cand - top), 0.0), axis=0, keepdims=True)
        self32 = jnp.where(sel, 1.0, 0.0)
        counts = [jnp.sum(self32[0:TOPK, :], axis=0, keepdims=True)]
        counts += [jnp.sum(self32[TOPK + 8 * (p - 1):TOPK + 8 * p, :], axis=0, keepdims=True) for p in range(1, 8)]
        tail = self32[TOPK + 56:TOPK + 64, :]
        counts += [jnp.sum(jnp.where(row8 == float(p), tail, 0.0), axis=0, keepdims=True) for p in range(8)]
        n1 = jnp.zeros((N_KEYS, lanes), F32)
        for p in range(TOPK):
            n1 = jnp.where(rank1 == float(p), counts[p], n1)
        n1_ref[h] = n1
        e1_ref[h] = jnp.exp(s1 - a_vals[0])
        r2_ref[h] = rank2
        e2_ref[h] = jnp.exp(s2 - b_vals[0]) / z


def _peer_route(h1b, wqt, keys):
    n = h1b.shape[0]
    maps = pl.BlockSpec((PEER_HEADS, N_KEYS, TB_ROUTE), lambda i: (0, 0, i))
    shape = jax.ShapeDtypeStruct((PEER_HEADS, N_KEYS, n), F32)
    return pl.pallas_call(
        _peer_route_kernel,
        grid=(n // TB_ROUTE,),
        in_specs=[pl.BlockSpec((TB_ROUTE, D_MODEL), lambda i: (i, 0)),
                  pl.BlockSpec((2 * PEER_HEADS * HALF, D_MODEL), lambda i: (0, 0)),
                  pl.BlockSpec((2 * PEER_HEADS, N_KEYS, HALF), lambda i: (0, 0, 0))],
        out_specs=[maps] * 4,
        out_shape=[shape] * 4,
        compiler_params=_params(("parallel",)),
        name="peer_route",
    )(h1b, wqt, keys)


def _peer_dense_kernel(xb_ref, x_ref, u_ref, vt_ref, n1_ref, e1_ref, r2_ref, e2_ref, lg_ref, lb_ref,
                       y_ref, acc_ref, act_ref):
    c = pl.program_id(1)

    @pl.when(c == 0)
    def _():
        acc_ref[...] = jnp.zeros_like(acc_ref)

    xb = xb_ref[...]
    for ii in range(I_PER_CHUNK):
        ht = lax.dot_general(u_ref[ii * N_KEYS:(ii + 1) * N_KEYS, :], xb, (((1,), (1,)), ((), ())),
                             preferred_element_type=F32)
        gate = jnp.zeros(ht.shape, F32)
        for h in range(PEER_HEADS):
            n1 = n1_ref[h, 0, ii:ii + 1, :]
            e1 = e1_ref[h, 0, ii:ii + 1, :]
            gate = gate + jnp.where(r2_ref[h] < n1, e1 * e2_ref[h], 0.0)
        act = (0.5 * ht) * (1.0 + lax.erf(ht * SQRT_HALF)) * gate
        act_ref[ii * N_KEYS:(ii + 1) * N_KEYS, :] = act.astype(BF16)
    acc_ref[...] += jnp.dot(vt_ref[...], act_ref[...], preferred_element_type=F32)

    @pl.when(c == pl.num_programs(1) - 1)
    def _():
        y_ref[...] = _layer_norm(ALPHA * x_ref[...] + acc_ref[...].T, lg_ref[...], lb_ref[...])


def _peer_dense(h1b, h1, u, vt, n1, e1, r2, e2, lg, lb):
    n = h1b.shape[0]
    n_chunks = N_EXPERTS // E_CHUNK
    tok = pl.BlockSpec((TB_PEER, D_MODEL), lambda i, c: (i, 0))
    by_first = pl.BlockSpec((PEER_HEADS, 1, I_PER_CHUNK, TB_PEER), lambda i, c: (0, c, 0, i))
    by_second = pl.BlockSpec((PEER_HEADS, N_KEYS, TB_PEER), lambda i, c: (0, 0, i))
    vec = pl.BlockSpec((1, D_MODEL), lambda i, c: (0, 0))
    return pl.pallas_call(
        _peer_dense_kernel,
        grid=(n // TB_PEER, n_chunks),
        in_specs=[tok, tok,
                  pl.BlockSpec((E_CHUNK, D_MODEL), lambda i, c: (c, 0)),
                  pl.BlockSpec((D_MODEL, E_CHUNK), lambda i, c: (0, c)),
                  by_first, by_first, by_second, by_second, vec, vec],
        out_specs=tok,
        out_shape=jax.ShapeDtypeStruct((n, D_MODEL), F32),
        scratch_shapes=[pltpu.VMEM((D_MODEL, TB_PEER), F32), pltpu.VMEM((E_CHUNK, TB_PEER), BF16)],
        compiler_params=_params(("parallel", "arbitrary")),
        name="peer_dense",
    )(h1b, h1, u, vt, n1, e1, r2, e2, lg, lb)


def _pad_cols(x, width):
    return jnp.pad(x, ((0, 0), (0, width - x.shape[1])))


def _pad_rows(x, height):
    return jnp.pad(x, ((0, height - x.shape[0]), (0, 0)))


def kernel(x_prompt, x_sample, state_conv, state_shift, state_wkv, meta_tokens, ln_in_g, ln_in_b, w_in, conv_w, mu, w0, w_up, a0, a_up, g_up, k_k, k_a, r_k, gn_g, gn_b, w_o, ln1_g, ln1_b, w_q, sub_keys, peer_u, peer_v, ln2_g, ln2_b):
    row = lambda x: x.reshape(1, -1)
    w_in0 = w_in[0]
    wcv = w_in0[:, :CV_COLS].astype(BF16)
    rw0 = CV_COLS
    lo0 = rw0 + 3 * RWKV_WIDTH

    def rw_layout(x):
        return jnp.concatenate([x[:, rw0:lo0],
                                _pad_cols(x[:, lo0:lo0 + W_LORA], LORA_SLOT),
                                _pad_cols(x[:, lo0 + W_LORA:lo0 + W_LORA + A_LORA], LORA_SLOT),
                                _pad_cols(x[:, lo0 + W_LORA + A_LORA:], LORA_SLOT)], axis=1)

    wrw = rw_layout(w_in0).astype(BF16)
    mu_p = rw_layout(jnp.pad(mu, ((0, 0), (CV_COLS, 0))))
    ones_bd = jnp.asarray(np.kron(np.eye(HEADS, dtype=np.float32), np.ones((HEAD, HEAD), np.float32)), BF16)
    mix_params = (conv_w[0], mu_p, w0, _pad_rows(w_up[0], LORA_SLOT).astype(BF16), a0,
                  _pad_rows(a_up[0], LORA_SLOT).astype(BF16), _pad_rows(g_up[0], LORA_SLOT).astype(BF16),
                  k_k, k_a, r_k.reshape(1, RWKV_WIDTH), ones_bd)
    out_params = (gn_g, gn_b, ones_bd, w_o[0, :CONV_WIDTH].astype(BF16), w_o[0, CONV_WIDTH:].astype(BF16),
                  ln1_g, ln1_b)

    meta = jnp.broadcast_to(meta_tokens[None], (BATCH, N_META, D_MODEL))
    x_all = jnp.concatenate([jnp.concatenate([meta, x_prompt], axis=1).reshape(T_P, D_MODEL),
                             x_sample.reshape(DEC_BATCH, D_MODEL)], axis=0)
    h, pcv, prw = _ln_proj(x_all, row(ln_in_g), row(ln_in_b), wcv, wrw)
    prev_rw_s = _shift_proj(state_shift[0], wrw)

    outs_p = _mixer_seq(pcv, prw, jnp.zeros((BATCH, 8, CONV_WIDTH), F32), jnp.zeros((BATCH, 8, RW_PAD), F32),
                        mix_params)
    conv_p = outs_p[N_MIX_OUT][:, 6:8]
    outs_s = _mixer_rows(pcv[T_P:], prw[T_P:], prev_rw_s, state_conv[0, :, 1], state_conv[0, :, 0], mix_params)
    z_s = outs_s[N_MIX_OUT]

    def scan(outs, nb, s_len, s0, steps):
        _, r, dec, kx, v, nkk, bb, _, _ = outs[:N_MIX_OUT]
        keys = [_to_scan_keys(x, nb, s_len) for x in (r, dec, kx)]
        nkk_t, bb_t = _to_scan_keys(nkk, nb, s_len), _to_scan_keys(bb, nb, s_len)
        o, st = _wkv_scan(keys[0], keys[1], keys[2], _to_scan_vals(v, nb, s_len), nkk_t, bb_t, s0, steps)
        return _from_scan_vals(o, nb, s_len), _state_from_scan(st, nb)

    o_p, wkv_p = scan(outs_p, BATCH, S_P, jnp.zeros((1, V_ROWS, HEAD, LANES), F32), TS_SCAN)
    o_s, wkv_s = scan(outs_s, DEC_BATCH, 1, _state_to_scan(state_wkv[0], DEC_BATCH), 1)

    h1_p, h1b_p = _mix_out(o_p, outs_p[7], outs_p[8], outs_p[0], h, 0, TB_OUT, out_params)
    h1_s, h1b_s = _mix_out(o_s, outs_s[7], outs_s[8], outs_s[0], h, T_P // DEC_BATCH, DEC_BATCH, out_params)
    h1 = jnp.concatenate([h1_p, h1_s], axis=0)
    h1b = jnp.concatenate([h1b_p, h1b_s], axis=0)

    wqt = w_q[0].T.astype(BF16)
    keys = sub_keys[0].reshape(2 * PEER_HEADS, N_KEYS, HALF).astype(BF16)
    n1, e1, r2, e2 = _peer_route(h1b, wqt, keys)
    by_first = lambda x: x.reshape(PEER_HEADS, N_KEYS // I_PER_CHUNK, I_PER_CHUNK, T_ALL)
    y = _peer_dense(h1b, h1, peer_u[0].astype(BF16), peer_v[0].T.astype(BF16), by_first(n1), by_first(e1), r2, e2,
                    row(ln2_g), row(ln2_b))

    y_prompt = y[:T_P].reshape(BATCH, S_P, D_MODEL)[:, N_META:]
    y_sample = y[T_P:].reshape(DEC_BATCH, 1, D_MODEL)
    h_p = h[:T_P].reshape(BATCH, S_P, D_MODEL)
    conv_s = jnp.stack([state_conv[0, :, 1], z_s], axis=1)
    return (y_prompt, y_sample, conv_p[None], h_p[:, -1][None], wkv_p[None],
            conv_s[None], h[T_P:][None], wkv_s[None])
```

```python
import functools

import numpy as np
import jax
import jax.numpy as jnp
from jax import lax
from jax.experimental import pallas as pl
from jax.experimental.pallas import tpu as pltpu

F32 = jnp.float32
BF16 = jnp.bfloat16

D_MODEL = 1024
N_META = 16
BATCH = 8
SEQ = 2048
S_P = SEQ + N_META
T_P = BATCH * S_P
DEC_BATCH = 128
T_ALL = T_P + DEC_BATCH
T_PEER = 16896
CONV_WIDTH = 512
RWKV_WIDTH = 512
HEAD = 64
HEADS = 8
W_LORA = 32
A_LORA = 32
G_LORA = 96
CV_COLS = 3 * CONV_WIDTH
LORA_SLOT = 128
RW_PAD = 3 * RWKV_WIDTH + 3 * LORA_SLOT
PEER_HEADS = 8
N_KEYS = 128
N_EXPERTS = N_KEYS * N_KEYS
TOPK = 16
HALF = 128
LN_EPS = 1e-5
GN_EPS = HEAD * 1e-5
ALPHA = 2.0 ** 0.25
SQRT_HALF = float(np.sqrt(0.5))
NOT_RANKED = 256.0
VMEM_LIMIT_BYTES = 56 * 1024 * 1024

TB_PROJ = 256
TB_MIX = 344
TB_OUT = 384
TS_SCAN = 48
TB_ROUTE = 256
TB_PEER = 512
I_PER_CHUNK = 16
E_CHUNK = I_PER_CHUNK * N_KEYS
I_PER_SUB = 4
E_SUB = I_PER_SUB * N_KEYS


def _params(sem):
    return pltpu.CompilerParams(dimension_semantics=sem, vmem_limit_bytes=VMEM_LIMIT_BYTES)


def _layer_norm(x, g, b):
    m = jnp.mean(x, axis=-1, keepdims=True)
    xc = x - m
    var = jnp.mean(xc * xc, axis=-1, keepdims=True)
    return xc * lax.rsqrt(var + LN_EPS) * g + b


def _headsum(x, ones_bd):
    hi = x.astype(BF16)
    lo = (x - hi.astype(F32)).astype(BF16)
    return (jnp.dot(hi, ones_bd, preferred_element_type=F32)
            + jnp.dot(lo, ones_bd, preferred_element_type=F32))


def _ln_proj_kernel(x_ref, g_ref, b_ref, wcv_ref, wrw_ref, h_ref, pcv_ref, prw_ref):
    h = _layer_norm(x_ref[...], g_ref[...], b_ref[...])
    h_ref[...] = h
    hb = h.astype(BF16)
    pcv_ref[...] = jnp.dot(hb, wcv_ref[...], preferred_element_type=F32)
    prw_ref[...] = jnp.dot(hb, wrw_ref[...], preferred_element_type=F32)


def _ln_proj(x_all, g, b, wcv, wrw):
    n = x_all.shape[0]
    row = lambda i: (i, 0)
    fixed = lambda i: (0, 0)
    return pl.pallas_call(
        _ln_proj_kernel,
        grid=(n // TB_PROJ,),
        in_specs=[pl.BlockSpec((TB_PROJ, D_MODEL), row),
                  pl.BlockSpec((1, D_MODEL), fixed),
                  pl.BlockSpec((1, D_MODEL), fixed),
                  pl.BlockSpec((D_MODEL, CV_COLS), fixed),
                  pl.BlockSpec((D_MODEL, RW_PAD), fixed)],
        out_specs=[pl.BlockSpec((TB_PROJ, D_MODEL), row),
                   pl.BlockSpec((TB_PROJ, CV_COLS), row),
                   pl.BlockSpec((TB_PROJ, RW_PAD), row)],
        out_shape=[jax.ShapeDtypeStruct((n, D_MODEL), F32),
                   jax.ShapeDtypeStruct((n, CV_COLS), F32),
                   jax.ShapeDtypeStruct((n, RW_PAD), F32)],
        compiler_params=_params(("parallel",)),
        name="ln_proj",
    )(x_all, g, b, wcv, wrw)


def _shift_proj_kernel(x_ref, wrw_ref, prw_ref):
    prw_ref[...] = jnp.dot(x_ref[...].astype(BF16), wrw_ref[...], preferred_element_type=F32)


def _shift_proj(x, wrw):
    n = x.shape[0]
    return pl.pallas_call(
        _shift_proj_kernel,
        out_shape=jax.ShapeDtypeStruct((n, RW_PAD), F32),
        compiler_params=_params(None),
        name="shift_proj",
    )(x, wrw)


N_MIX_OUT = 9


def _mixer_core(pcv, prw, prw_prev, z1, z2, cw, mu, w0, wup, a0, aup, gup, kkw, kaw, rkw, ones_bd):
    bg = pcv[:, 0:CONV_WIDTH]
    z = pcv[:, CONV_WIDTH:2 * CONV_WIDTH] * pcv[:, 2 * CONV_WIDTH:3 * CONV_WIDTH]
    conv = cw[0:1, :] * z2 + cw[1:2, :] * z1 + cw[2:3, :] * z
    y_cv = bg * conv

    m = prw + (prw_prev - prw) * mu
    r = m[:, 0:RWKV_WIDTH]
    k = m[:, RWKV_WIDTH:2 * RWKV_WIDTH]
    v = m[:, 2 * RWKV_WIDTH:3 * RWKV_WIDTH]
    base = 3 * RWKV_WIDTH
    wd = m[:, base:base + LORA_SLOT]
    ad = m[:, base + LORA_SLOT:base + 2 * LORA_SLOT]
    gd = m[:, base + 2 * LORA_SLOT:base + 3 * LORA_SLOT]

    xw = -(w0 + jnp.dot(jnp.tanh(wd).astype(BF16), wup, preferred_element_type=F32))
    softplus = jnp.maximum(xw, 0.0) + jnp.log1p(jnp.exp(-jnp.abs(xw)))
    w_log = -softplus - 0.5
    decay = jnp.exp(-jnp.exp(w_log))
    a = jax.nn.sigmoid(a0 + jnp.dot(ad.astype(BF16), aup, preferred_element_type=F32))
    g = jnp.dot(jax.nn.sigmoid(gd).astype(BF16), gup, preferred_element_type=F32)

    kk = k * kkw
    norm = jnp.sqrt(_headsum(kk * kk, ones_bd))
    kk = kk / jnp.maximum(norm, 1e-12)
    kx = k * (1.0 + (a - 1.0) * kaw)
    bonus_v = _headsum(r * kx * rkw, ones_bd) * v
    return y_cv, z, (y_cv, r, decay, kx, v, -kk, kk * a, g, bonus_v)


def _mixer_seq_kernel(pcv_ref, prw_ref, conv0_ref, prw0_ref, cw_ref, mu_ref, w0_ref, wup_ref, a0_ref, aup_ref,
                      gup_ref, kkw_ref, kaw_ref, rkw_ref, ones_ref, *rest):
    out_refs = rest[:N_MIX_OUT]
    convt_ref = rest[N_MIX_OUT]
    zc_ref, pc_ref = rest[N_MIX_OUT + 1:]
    tb = pl.program_id(1)

    @pl.when(tb == 0)
    def _():
        zc_ref[...] = conv0_ref[0]
        pc_ref[...] = prw0_ref[0]

    pcv = pcv_ref[...]
    prw = prw_ref[...]
    n = pcv.shape[0]
    z = pcv[:, CONV_WIDTH:2 * CONV_WIDTH] * pcv[:, 2 * CONV_WIDTH:3 * CONV_WIDTH]
    row_z = lax.broadcasted_iota(jnp.int32, z.shape, 0)
    row_p = lax.broadcasted_iota(jnp.int32, prw.shape, 0)
    c2 = zc_ref[6:7, :]
    c1 = zc_ref[7:8, :]
    z1 = jnp.where(row_z == 0, c1, pltpu.roll(z, 1, axis=0))
    z2 = jnp.where(row_z == 0, c2, jnp.where(row_z == 1, c1, pltpu.roll(z, 2, axis=0)))
    prw_prev = jnp.where(row_p == 0, pc_ref[7:8, :], pltpu.roll(prw, 1, axis=0))

    _, _, outs = _mixer_core(pcv, prw, prw_prev, z1, z2, cw_ref[...], mu_ref[...], w0_ref[...], wup_ref[...],
                             a0_ref[...], aup_ref[...], gup_ref[...], kkw_ref[...], kaw_ref[...], rkw_ref[...],
                             ones_ref[...])
    for ref, val in zip(out_refs, outs):
        ref[...] = val
    tail = z[n - 8:n, :]
    zc_ref[...] = tail
    pc_ref[...] = prw[n - 8:n, :]
    convt_ref[0] = tail


def _mixer_row_kernel(pcv_ref, prw_ref, prwprev_ref, z1_ref, z2_ref, cw_ref, mu_ref, w0_ref, wup_ref, a0_ref,
                      aup_ref, gup_ref, kkw_ref, kaw_ref, rkw_ref, ones_ref, *rest):
    out_refs = rest[:N_MIX_OUT]
    z_ref = rest[N_MIX_OUT]
    _, z, outs = _mixer_core(pcv_ref[...], prw_ref[...], prwprev_ref[...], z1_ref[...], z2_ref[...], cw_ref[...],
                             mu_ref[...], w0_ref[...], wup_ref[...], a0_ref[...], aup_ref[...], gup_ref[...],
                             kkw_ref[...], kaw_ref[...], rkw_ref[...], ones_ref[...])
    for ref, val in zip(out_refs, outs):
        ref[...] = val
    z_ref[...] = z


def _mixer_param_specs(nargs_grid):
    fixed = (lambda *_: (0, 0))
    shapes = [(3, CONV_WIDTH), (1, RW_PAD), (1, RWKV_WIDTH), (LORA_SLOT, RWKV_WIDTH), (1, RWKV_WIDTH),
              (LORA_SLOT, RWKV_WIDTH), (LORA_SLOT, RWKV_WIDTH), (1, RWKV_WIDTH), (1, RWKV_WIDTH), (1, RWKV_WIDTH),
              (RWKV_WIDTH, RWKV_WIDTH)]
    return [pl.BlockSpec(s, fixed) for s in shapes]


def _mixer_seq(pcv, prw, conv0, prw0, mix_params):
    nb = S_P // TB_MIX
    row = lambda b, t: (b * nb + t, 0)
    seq = lambda b, t: (b, 0, 0)
    out512 = pl.BlockSpec((TB_MIX, RWKV_WIDTH), row)
    return pl.pallas_call(
        _mixer_seq_kernel,
        grid=(BATCH, nb),
        in_specs=[pl.BlockSpec((TB_MIX, CV_COLS), row),
                  pl.BlockSpec((TB_MIX, RW_PAD), row),
                  pl.BlockSpec((1, 8, CONV_WIDTH), seq),
                  pl.BlockSpec((1, 8, RW_PAD), seq)] + _mixer_param_specs(2),
        out_specs=[out512] * N_MIX_OUT + [pl.BlockSpec((1, 8, CONV_WIDTH), seq)],
        out_shape=[jax.ShapeDtypeStruct((T_P, RWKV_WIDTH), F32)] * N_MIX_OUT
                  + [jax.ShapeDtypeStruct((BATCH, 8, CONV_WIDTH), F32)],
        scratch_shapes=[pltpu.VMEM((8, CONV_WIDTH), F32), pltpu.VMEM((8, RW_PAD), F32)],
        compiler_params=_params(("arbitrary", "arbitrary")),
        name="mixer_seq",
    )(pcv, prw, conv0, prw0, *mix_params)


def _mixer_rows(pcv, prw, prw_prev, z1, z2, mix_params):
    n = pcv.shape[0]
    return pl.pallas_call(
        _mixer_row_kernel,
        out_shape=[jax.ShapeDtypeStruct((n, RWKV_WIDTH), F32)] * (N_MIX_OUT + 1),
        compiler_params=_params(None),
        name="mixer_rows",
    )(pcv, prw, prw_prev, z1, z2, *mix_params)


V_ROWS = HEAD // 2
LANES = 128


def _wkv_kernel(r_ref, w_ref, k_ref, v_ref, nkk_ref, bb_ref, s0_ref, o_ref, st_ref, s_scr, *, steps):
    tc = pl.program_id(1)

    @pl.when(tc == 0)
    def _():
        s_scr[...] = s0_ref[0]

    def step(t, carry):
        r = r_ref[0, t]
        w = w_ref[0, t]
        kx = k_ref[0, t]
        nkk = nkk_ref[0, t]
        bb = bb_ref[0, t]
        for vl in range(V_ROWS):
            sv = s_scr[vl]
            sa = jnp.sum(sv * nkk, axis=0, keepdims=True)
            vrow = v_ref[0, t, pl.ds(vl, 1), :]
            sn = sv * w + sa * bb + vrow * kx
            s_scr[vl] = sn
            o_ref[0, t, pl.ds(vl, 1), :] = jnp.sum(sn * r, axis=0, keepdims=True)
        return carry

    lax.fori_loop(0, steps, step, 0)

    @pl.when(tc == pl.num_programs(1) - 1)
    def _():
        st_ref[0] = s_scr[...]


def _wkv_scan(r, w, k, v, nkk, bb, s0, steps):
    groups, s_len = r.shape[0], r.shape[1]
    key_spec = pl.BlockSpec((1, steps, HEAD, LANES), lambda g, t: (g, t, 0, 0))
    val_spec = pl.BlockSpec((1, steps, V_ROWS, LANES), lambda g, t: (g, t, 0, 0))
    st_spec = pl.BlockSpec((1, V_ROWS, HEAD, LANES), lambda g, t: (g, 0, 0, 0))
    return pl.pallas_call(
        functools.partial(_wkv_kernel, steps=steps),
        grid=(groups, s_len // steps),
        in_specs=[key_spec, key_spec, key_spec, val_spec, key_spec, key_spec, st_spec],
        out_specs=[val_spec, st_spec],
        out_shape=[jax.ShapeDtypeStruct((groups, s_len, V_ROWS, LANES), F32),
                   jax.ShapeDtypeStruct((groups, V_ROWS, HEAD, LANES), F32)],
        scratch_shapes=[pltpu.VMEM((V_ROWS, HEAD, LANES), F32)],
        compiler_params=_params(("arbitrary", "arbitrary")),
        name="wkv_scan",
    )(r, w, k, v, nkk, bb, s0)


def _to_scan_keys(x, nb, s_len):
    g = nb // BATCH
    x = x.reshape(g, BATCH, s_len, HEADS, HEAD).transpose(0, 2, 4, 1, 3).reshape(g, s_len, HEAD, BATCH * HEADS)
    return jnp.concatenate([x, x], axis=-1)


def _to_scan_vals(x, nb, s_len):
    g = nb // BATCH
    x = x.reshape(g, BATCH, s_len, HEADS, 2, V_ROWS).transpose(0, 2, 5, 4, 1, 3)
    return x.reshape(g, s_len, V_ROWS, LANES)


def _from_scan_vals(o, nb, s_len):
    g = nb // BATCH
    o = o.reshape(g, s_len, V_ROWS, 2, BATCH, HEADS).transpose(0, 4, 1, 5, 3, 2)
    return o.reshape(nb * s_len, RWKV_WIDTH)


def _state_to_scan(s, nb):
    g = nb // BATCH
    s = s.reshape(g, BATCH, HEADS, 2, V_ROWS, HEAD).transpose(0, 4, 5, 3, 1, 2)
    return s.reshape(g, V_ROWS, HEAD, LANES)


def _state_from_scan(s, nb):
    g = nb // BATCH
    s = s.reshape(g, V_ROWS, HEAD, 2, BATCH, HEADS).transpose(0, 4, 5, 3, 1, 2)
    return s.reshape(nb, HEADS, HEAD, HEAD)


def _mix_out_kernel(o_ref, g_ref, bv_ref, ycv_ref, h_ref, gng_ref, gnb_ref, ones_ref, wo1_ref, wo2_ref,
                    lg_ref, lb_ref, h1_ref, h1b_ref):
    ones_bd = ones_ref[...]
    o = o_ref[...]
    d = o - _headsum(o, ones_bd) * (1.0 / HEAD)
    var = _headsum(d * d, ones_bd) * (1.0 / HEAD)
    gn = d * lax.rsqrt(var + GN_EPS) * gng_ref[...] + gnb_ref[...]
    y_rw = (gn + bv_ref[...]) * g_ref[...]
    mix = (jnp.dot(ycv_ref[...].astype(BF16), wo1_ref[...], preferred_element_type=F32)
           + jnp.dot(y_rw.astype(BF16), wo2_ref[...], preferred_element_type=F32))
    h1 = _layer_norm(ALPHA * h_ref[...] + mix, lg_ref[...], lb_ref[...])
    h1_ref[...] = h1
    h1b_ref[...] = h1.astype(BF16)


def _mix_out(o, g, bv, ycv, h, h_block0, tb, out_params):
    n = o.shape[0]
    row = lambda i: (i, 0)
    hrow = lambda i: (i + h_block0, 0)
    fixed = lambda i: (0, 0)
    s512 = pl.BlockSpec((tb, RWKV_WIDTH), row)
    return pl.pallas_call(
        _mix_out_kernel,
        grid=(n // tb,),
        in_specs=[s512, s512, s512, s512, pl.BlockSpec((tb, D_MODEL), hrow),
                  pl.BlockSpec((1, RWKV_WIDTH), fixed), pl.BlockSpec((1, RWKV_WIDTH), fixed),
                  pl.BlockSpec((RWKV_WIDTH, RWKV_WIDTH), fixed),
                  pl.BlockSpec((CONV_WIDTH, D_MODEL), fixed), pl.BlockSpec((RWKV_WIDTH, D_MODEL), fixed),
                  pl.BlockSpec((1, D_MODEL), fixed), pl.BlockSpec((1, D_MODEL), fixed)],
        out_specs=[pl.BlockSpec((tb, D_MODEL), row), pl.BlockSpec((tb, D_MODEL), row)],
        out_shape=[jax.ShapeDtypeStruct((n, D_MODEL), F32), jax.ShapeDtypeStruct((n, D_MODEL), BF16)],
        compiler_params=_params(("parallel",)),
        name="mix_out",
    )(o, g, bv, ycv, h, *out_params)


CAND_ROWS = TOPK + 7 * 8 + 8


def _extract_top(s, ids, count):
    rank = jnp.full(s.shape, NOT_RANKED, F32)
    big = jnp.float32(NOT_RANKED)
    vals, picks = [], []
    for p in range(count):
        m = jnp.max(s, axis=0, keepdims=True)
        pick = jnp.min(jnp.where(s == m, ids, big), axis=0, keepdims=True)
        hit = ids == pick
        rank = jnp.where(hit, jnp.float32(p), rank)
        s = jnp.where(hit, -jnp.inf, s)
        vals.append(m)
        picks.append(pick)
    return vals, picks, rank


def _peer_route_kernel(x_ref, wqt_ref, keys_ref, n1_ref, e1_ref, r2_ref, e2_ref):
    xb = x_ref[...]
    qt = lax.dot_general(wqt_ref[...], xb, (((1,), (1,)), ((), ())), preferred_element_type=F32)
    lanes = xb.shape[0]
    key_ids = lax.broadcasted_iota(jnp.int32, (N_KEYS, lanes), 0).astype(F32)
    row16 = lax.broadcasted_iota(jnp.int32, (TOPK, lanes), 0).astype(F32)
    row8 = lax.broadcasted_iota(jnp.int32, (8, lanes), 0).astype(F32)
    cand_ids = jnp.concatenate([row16] + [row8 + float(TOPK * p) for p in range(1, 8)]
                               + [(row8 + 8.0) * float(TOPK)], axis=0)
    for h in range(PEER_HEADS):
        s1 = jnp.dot(keys_ref[2 * h], qt[(2 * h) * HALF:(2 * h + 1) * HALF, :].astype(BF16),
                     preferred_element_type=F32)
        s2 = jnp.dot(keys_ref[2 * h + 1], qt[(2 * h + 1) * HALF:(2 * h + 2) * HALF, :].astype(BF16),
                     preferred_element_type=F32)
        a_vals, _, rank1 = _extract_top(s1, key_ids, TOPK)
        b_vals, _, rank2 = _extract_top(s2, key_ids, TOPK)
        b = jnp.zeros((TOPK, lanes), F32)
        a_hi = jnp.zeros((8, lanes), F32)
        for p in range(TOPK):
            b = jnp.where(row16 == float(p), b_vals[p], b)
        for p in range(8):
            a_hi = jnp.where(row8 == float(p), a_vals[8 + p], a_hi)
        cand = jnp.concatenate([a_vals[0] + b] + [a_vals[p] + b[0:8, :] for p in range(1, 8)]
                               + [a_hi + b_vals[0]], axis=0)
        _, _, crank = _extract_top(cand, cand_ids, TOPK)
        sel = crank < float(TOPK)
        top = a_vals[0] + b_vals[0]
        z = jnp.sum(jnp.where(sel, jnp.exp(cand - top), 0.0), axis=0, keepdims=True)
        self32 = jnp.where(sel, 1.0, 0.0)
        counts = [jnp.sum(self32[0:TOPK, :], axis=0, keepdims=True)]
        counts += [jnp.sum(self32[TOPK + 8 * (p - 1):TOPK + 8 * p, :], axis=0, keepdims=True) for p in range(1, 8)]
        tail = self32[TOPK + 56:TOPK + 64, :]
        counts += [jnp.sum(jnp.where(row8 == float(p), tail, 0.0), axis=0, keepdims=True) for p in range(8)]
        n1 = jnp.zeros((N_KEYS, lanes), F32)
        for p in range(TOPK):
            n1 = jnp.where(rank1 == float(p), counts[p], n1)
        n1_ref[h] = n1
        e1_ref[h] = jnp.exp(s1 - a_vals[0])
        r2_ref[h] = rank2.astype(BF16)
        e2_ref[h] = (jnp.exp(s2 - b_vals[0]) * (0.5 / z)).astype(BF16)


def _peer_route(h1b, wqt, keys):
    n = h1b.shape[0]
    maps = pl.BlockSpec((PEER_HEADS, N_KEYS, TB_ROUTE), lambda i: (0, 0, i))
    f32_map = jax.ShapeDtypeStruct((PEER_HEADS, N_KEYS, n), F32)
    bf16_map = jax.ShapeDtypeStruct((PEER_HEADS, N_KEYS, n), BF16)
    return pl.pallas_call(
        _peer_route_kernel,
        grid=(n // TB_ROUTE,),
        in_specs=[pl.BlockSpec((TB_ROUTE, D_MODEL), lambda i: (i, 0)),
                  pl.BlockSpec((2 * PEER_HEADS * HALF, D_MODEL), lambda i: (0, 0)),
                  pl.BlockSpec((2 * PEER_HEADS, N_KEYS, HALF), lambda i: (0, 0, 0))],
        out_specs=[maps] * 4,
        out_shape=[f32_map, f32_map, bf16_map, bf16_map],
        compiler_params=_params(("parallel",)),
        name="peer_route",
    )(h1b, wqt, keys)


SLAB = 16


def _peer_dense_kernel(xb_ref, x_ref, u_ref, vt_ref, n1_ref, e1_ref, r2_ref, e2_ref, lg_ref, lb_ref,
                       y_ref, acc_ref, ht_ref, act_ref):
    c = pl.program_id(1)

    @pl.when(c == 0)
    def _():
        acc_ref[...] = jnp.zeros_like(acc_ref)

    lanes = ht_ref.shape[1]
    zero = jnp.zeros((SLAB, lanes), BF16)
    xb = xb_ref[...]
    for sub in range(I_PER_CHUNK // I_PER_SUB):
        sub_rows = slice(sub * E_SUB, (sub + 1) * E_SUB)
        ht_ref[sub_rows, :] = lax.dot_general(u_ref[sub_rows, :], xb, (((1,), (1,)), ((), ())),
                                              preferred_element_type=F32)
    for sub in range(I_PER_CHUNK // I_PER_SUB):
        sub_rows = slice(sub * E_SUB, (sub + 1) * E_SUB)
        for ii in range(sub * I_PER_SUB, (sub + 1) * I_PER_SUB):
            n1 = [jnp.broadcast_to(n1_ref[h, 0, ii:ii + 1, :], (SLAB, lanes)).astype(BF16)
                  for h in range(PEER_HEADS)]
            e1 = [jnp.broadcast_to(e1_ref[h, 0, ii:ii + 1, :], (SLAB, lanes)).astype(BF16)
                  for h in range(PEER_HEADS)]
            for js in range(N_KEYS // SLAB):
                rows = slice(js * SLAB, (js + 1) * SLAB)
                gate = jnp.where(r2_ref[0, rows, :] < n1[0], e2_ref[0, rows, :], zero) * e1[0]
                for h in range(1, PEER_HEADS):
                    gate = gate + jnp.where(r2_ref[h, rows, :] < n1[h], e2_ref[h, rows, :], zero) * e1[h]
                out_rows = slice(ii * N_KEYS + js * SLAB, ii * N_KEYS + (js + 1) * SLAB)
                ht = ht_ref[out_rows, :]
                gelu2 = ht * (1.0 + lax.erf(ht * SQRT_HALF))
                act_ref[out_rows, :] = gelu2.astype(BF16) * gate
        acc_ref[...] += jnp.dot(vt_ref[:, sub_rows], act_ref[sub_rows, :], preferred_element_type=F32)

    @pl.when(c == pl.num_programs(1) - 1)
    def _():
        y_ref[...] = _layer_norm(ALPHA * x_ref[...] + acc_ref[...].T, lg_ref[...], lb_ref[...])


def _peer_dense(h1b, h1, u, vt, n1, e1, r2, e2, lg, lb):
    n = h1b.shape[0]
    n_chunks = N_EXPERTS // E_CHUNK
    tok = pl.BlockSpec((TB_PEER, D_MODEL), lambda i, c: (i, 0))
    by_first = pl.BlockSpec((PEER_HEADS, 1, I_PER_CHUNK, TB_PEER), lambda i, c: (0, c, 0, i))
    by_second = pl.BlockSpec((PEER_HEADS, N_KEYS, TB_PEER), lambda i, c: (0, 0, i))
    vec = pl.BlockSpec((1, D_MODEL), lambda i, c: (0, 0))
    return pl.pallas_call(
        _peer_dense_kernel,
        grid=(n // TB_PEER, n_chunks),
        in_specs=[tok, tok,
                  pl.BlockSpec((E_CHUNK, D_MODEL), lambda i, c: (c, 0)),
                  pl.BlockSpec((D_MODEL, E_CHUNK), lambda i, c: (0, c)),
                  by_first, by_first, by_second, by_second, vec, vec],
        out_specs=tok,
        out_shape=jax.ShapeDtypeStruct((n, D_MODEL), F32),
        scratch_shapes=[pltpu.VMEM((D_MODEL, TB_PEER), F32), pltpu.VMEM((E_CHUNK, TB_PEER), F32),
                        pltpu.VMEM((E_CHUNK, TB_PEER), BF16)],
        compiler_params=_params(("parallel", "arbitrary")),
        name="peer_dense",
    )(h1b, h1, u, vt, n1, e1, r2, e2, lg, lb)


def _pad_cols(x, width):
    return jnp.pad(x, ((0, 0), (0, width - x.shape[1])))


def _pad_rows(x, height):
    return jnp.pad(x, ((0, height - x.shape[0]), (0, 0)))


def kernel(x_prompt, x_sample, state_conv, state_shift, state_wkv, meta_tokens, ln_in_g, ln_in_b, w_in, conv_w, mu, w0, w_up, a0, a_up, g_up, k_k, k_a, r_k, gn_g, gn_b, w_o, ln1_g, ln1_b, w_q, sub_keys, peer_u, peer_v, ln2_g, ln2_b):
    row = lambda x: x.reshape(1, -1)
    w_in0 = w_in[0]
    wcv = w_in0[:, :CV_COLS].astype(BF16)
    rw0 = CV_COLS
    lo0 = rw0 + 3 * RWKV_WIDTH

    def rw_layout(x):
        return jnp.concatenate([x[:, rw0:lo0],
                                _pad_cols(x[:, lo0:lo0 + W_LORA], LORA_SLOT),
                                _pad_cols(x[:, lo0 + W_LORA:lo0 + W_LORA + A_LORA], LORA_SLOT),
                                _pad_cols(x[:, lo0 + W_LORA + A_LORA:], LORA_SLOT)], axis=1)

    wrw = rw_layout(w_in0).astype(BF16)
    mu_p = rw_layout(jnp.pad(mu, ((0, 0), (CV_COLS, 0))))
    ones_bd = jnp.asarray(np.kron(np.eye(HEADS, dtype=np.float32), np.ones((HEAD, HEAD), np.float32)), BF16)
    mix_params = (conv_w[0], mu_p, w0, _pad_rows(w_up[0], LORA_SLOT).astype(BF16), a0,
                  _pad_rows(a_up[0], LORA_SLOT).astype(BF16), _pad_rows(g_up[0], LORA_SLOT).astype(BF16),
                  k_k, k_a, r_k.reshape(1, RWKV_WIDTH), ones_bd)
    out_params = (gn_g, gn_b, ones_bd, w_o[0, :CONV_WIDTH].astype(BF16), w_o[0, CONV_WIDTH:].astype(BF16),
                  ln1_g, ln1_b)

    meta = jnp.broadcast_to(meta_tokens[None], (BATCH, N_META, D_MODEL))
    x_all = jnp.concatenate([jnp.concatenate([meta, x_prompt], axis=1).reshape(T_P, D_MODEL),
                             x_sample.reshape(DEC_BATCH, D_MODEL)], axis=0)
    h, pcv, prw = _ln_proj(x_all, row(ln_in_g), row(ln_in_b), wcv, wrw)
    prev_rw_s = _shift_proj(state_shift[0], wrw)

    outs_p = _mixer_seq(pcv, prw, jnp.zeros((BATCH, 8, CONV_WIDTH), F32), jnp.zeros((BATCH, 8, RW_PAD), F32),
                        mix_params)
    conv_p = outs_p[N_MIX_OUT][:, 6:8]
    outs_s = _mixer_rows(pcv[T_P:], prw[T_P:], prev_rw_s, state_conv[0, :, 1], state_conv[0, :, 0], mix_params)
    z_s = outs_s[N_MIX_OUT]

    def scan(outs, nb, s_len, s0, steps):
        _, r, dec, kx, v, nkk, bb, _, _ = outs[:N_MIX_OUT]
        keys = [_to_scan_keys(x, nb, s_len) for x in (r, dec, kx)]
        nkk_t, bb_t = _to_scan_keys(nkk, nb, s_len), _to_scan_keys(bb, nb, s_len)
        o, st = _wkv_scan(keys[0], keys[1], keys[2], _to_scan_vals(v, nb, s_len), nkk_t, bb_t, s0, steps)
        return _from_scan_vals(o, nb, s_len), _state_from_scan(st, nb)

    o_p, wkv_p = scan(outs_p, BATCH, S_P, jnp.zeros((1, V_ROWS, HEAD, LANES), F32), TS_SCAN)
    o_s, wkv_s = scan(outs_s, DEC_BATCH, 1, _state_to_scan(state_wkv[0], DEC_BATCH), 1)

    h1_p, h1b_p = _mix_out(o_p, outs_p[7], outs_p[8], outs_p[0], h, 0, TB_OUT, out_params)
    h1_s, h1b_s = _mix_out(o_s, outs_s[7], outs_s[8], outs_s[0], h, T_P // DEC_BATCH, DEC_BATCH, out_params)
    pad = T_PEER - T_ALL
    h1 = jnp.concatenate([h1_p, h1_s, jnp.zeros((pad, D_MODEL), F32)], axis=0)
    h1b = jnp.concatenate([h1b_p, h1b_s, jnp.zeros((pad, D_MODEL), BF16)], axis=0)

    wqt = w_q[0].T.astype(BF16)
    keys = sub_keys[0].reshape(2 * PEER_HEADS, N_KEYS, HALF).astype(BF16)
    n1, e1, r2, e2 = _peer_route(h1b, wqt, keys)
    by_first = lambda x: x.reshape(PEER_HEADS, N_KEYS // I_PER_CHUNK, I_PER_CHUNK, T_PEER)
    y = _peer_dense(h1b, h1, peer_u[0].astype(BF16), peer_v[0].T.astype(BF16), by_first(n1), by_first(e1), r2, e2,
                    row(ln2_g), row(ln2_b))

    y_prompt = y[:T_P].reshape(BATCH, S_P, D_MODEL)[:, N_META:]
    y_sample = y[T_P:T_ALL].reshape(DEC_BATCH, 1, D_MODEL)
    h_p = h[:T_P].reshape(BATCH, S_P, D_MODEL)
    conv_s = jnp.stack([state_conv[0, :, 1], z_s], axis=1)
    return (y_prompt, y_sample, conv_p[None], h_p[:, -1][None], wkv_p[None],
            conv_s[None], h[T_P:][None], wkv_s[None])
```

```python
import functools

import numpy as np
import jax
import jax.numpy as jnp
from jax import lax
from jax.experimental import pallas as pl
from jax.experimental.pallas import tpu as pltpu

F32 = jnp.float32
BF16 = jnp.bfloat16

D_MODEL = 1024
N_META = 16
BATCH = 8
SEQ = 2048
DEC_BATCH = 128
T_BLK = 128
S_PITCH = 2304
N_TIME_BLK = SEQ // T_BLK + 1
META_BLK = S_PITCH // T_BLK - 1
T_SEQS = BATCH * S_PITCH
T_PHYS = T_SEQS + 2 * DEC_BATCH
T_REAL = BATCH * SEQ + DEC_BATCH
T_PEER = 16896
CONV_WIDTH = 512
RWKV_WIDTH = 512
HEAD = 64
HEADS = 8
W_LORA = 32
A_LORA = 32
G_LORA = 96
CV_COLS = 3 * CONV_WIDTH
LORA_SLOT = 128
RW_PAD = 3 * RWKV_WIDTH + 3 * LORA_SLOT
PEER_HEADS = 8
N_KEYS = 128
N_EXPERTS = N_KEYS * N_KEYS
TOPK = 16
HALF = 128
LN_EPS = 1e-5
GN_EPS = HEAD * 1e-5
ALPHA = 2.0 ** 0.25
SQRT_HALF = float(np.sqrt(0.5))
NOT_RANKED = 256.0
VMEM_LIMIT_BYTES = 56 * 1024 * 1024

TB_PROJ = 256
TB_OUT = 256
TB_ROUTE = 256
TB_PEER = 512
I_PER_CHUNK = 16
E_CHUNK = I_PER_CHUNK * N_KEYS
I_PER_SUB = 4
E_SUB = I_PER_SUB * N_KEYS


def _params(sem):
    return pltpu.CompilerParams(dimension_semantics=sem, vmem_limit_bytes=VMEM_LIMIT_BYTES)


def _layer_norm(x, g, b):
    m = jnp.mean(x, axis=-1, keepdims=True)
    xc = x - m
    var = jnp.mean(xc * xc, axis=-1, keepdims=True)
    return xc * lax.rsqrt(var + LN_EPS) * g + b


def _headsum(x, ones_hd):
    hi = x.astype(BF16)
    lo = (x - hi.astype(F32)).astype(BF16)
    return (jnp.dot(hi, ones_hd, preferred_element_type=F32)
            + jnp.dot(lo, ones_hd, preferred_element_type=F32))


def _time_block(l):
    return jnp.where(l == 0, META_BLK, l - 1)


def _ln_proj_kernel(x_ref, g_ref, b_ref, wcv_ref, wrw_ref, h_ref, pcv_ref, prw_ref):
    h = _layer_norm(x_ref[...], g_ref[...], b_ref[...])
    h_ref[...] = h
    hb = h.astype(BF16)
    pcv_ref[...] = jnp.dot(hb, wcv_ref[...], preferred_element_type=F32)
    prw_ref[...] = jnp.dot(hb, wrw_ref[...], preferred_element_type=F32)


def _ln_proj(x_all, g, b, wcv, wrw):
    n = x_all.shape[0]
    row = lambda i: (i, 0)
    fixed = lambda i: (0, 0)
    return pl.pallas_call(
        _ln_proj_kernel,
        grid=(n // TB_PROJ,),
        in_specs=[pl.BlockSpec((TB_PROJ, D_MODEL), row),
                  pl.BlockSpec((1, D_MODEL), fixed),
                  pl.BlockSpec((1, D_MODEL), fixed),
                  pl.BlockSpec((D_MODEL, CV_COLS), fixed),
                  pl.BlockSpec((D_MODEL, RW_PAD), fixed)],
        out_specs=[pl.BlockSpec((TB_PROJ, D_MODEL), row),
                   pl.BlockSpec((TB_PROJ, CV_COLS), row),
                   pl.BlockSpec((TB_PROJ, RW_PAD), row)],
        out_shape=[jax.ShapeDtypeStruct((n, D_MODEL), F32),
                   jax.ShapeDtypeStruct((n, CV_COLS), F32),
                   jax.ShapeDtypeStruct((n, RW_PAD), F32)],
        compiler_params=_params(("parallel",)),
        name="ln_proj",
    )(x_all, g, b, wcv, wrw)


def _shift_proj_kernel(x_ref, wrw_ref, prw_ref):
    prw_ref[...] = jnp.dot(x_ref[...].astype(BF16), wrw_ref[...], preferred_element_type=F32)


def _shift_proj(x, wrw):
    n = x.shape[0]
    return pl.pallas_call(
        _shift_proj_kernel,
        out_shape=jax.ShapeDtypeStruct((n, RW_PAD), F32),
        compiler_params=_params(None),
        name="shift_proj",
    )(x, wrw)


N_SCAN_IN = 6
N_TOK_OUT = 3


def _mixer_core(pcv, prw, prw_prev, z1, z2, cw, mu, w0, wup, a0, aup, gup, kkw, kaw, rkw, ones_hd):
    bg = pcv[:, 0:CONV_WIDTH]
    z = pcv[:, CONV_WIDTH:2 * CONV_WIDTH] * pcv[:, 2 * CONV_WIDTH:3 * CONV_WIDTH]
    conv = cw[0:1, :] * z2 + cw[1:2, :] * z1 + cw[2:3, :] * z
    y_cv = bg * conv

    m = prw + (prw_prev - prw) * mu
    r = m[:, 0:RWKV_WIDTH]
    k = m[:, RWKV_WIDTH:2 * RWKV_WIDTH]
    v = m[:, 2 * RWKV_WIDTH:3 * RWKV_WIDTH]
    base = 3 * RWKV_WIDTH
    wd = m[:, base:base + LORA_SLOT]
    ad = m[:, base + LORA_SLOT:base + 2 * LORA_SLOT]
    gd = m[:, base + 2 * LORA_SLOT:base + 3 * LORA_SLOT]

    xw = -(w0 + jnp.dot(jnp.tanh(wd).astype(BF16), wup, preferred_element_type=F32))
    softplus = jnp.maximum(xw, 0.0) + jnp.log1p(jnp.exp(-jnp.abs(xw)))
    w_log = -softplus - 0.5
    decay = jnp.exp(-jnp.exp(w_log))
    a = jax.nn.sigmoid(a0 + jnp.dot(ad.astype(BF16), aup, preferred_element_type=F32))
    g = jnp.dot(jax.nn.sigmoid(gd).astype(BF16), gup, preferred_element_type=F32)

    kk = k * kkw
    norm = jnp.sqrt(_headsum(kk * kk, ones_hd))
    kk = kk / jnp.maximum(norm, 1e-12)
    kx = k * (1.0 + (a - 1.0) * kaw)
    bonus_v = _headsum(r * kx * rkw, ones_hd) * v
    return z, (r, decay, kx, -kk, kk * a, v), (y_cv, g, bonus_v)


def _mixer_seq_kernel(pcv_ref, prw_ref, conv0_ref, prw0_ref, cw_ref, mu_ref, w0_ref, wup_ref, a0_ref, aup_ref,
                      gup_ref, kkw_ref, kaw_ref, rkw_ref, ones_ref, *rest):
    scan_refs = rest[:N_SCAN_IN]
    tok_refs = rest[N_SCAN_IN:N_SCAN_IN + N_TOK_OUT]
    convt_ref = rest[N_SCAN_IN + N_TOK_OUT]
    zc_ref, pc_ref = rest[N_SCAN_IN + N_TOK_OUT + 1:]
    l = pl.program_id(1)

    @pl.when(l == 0)
    def _():
        zc_ref[...] = conv0_ref[0]
        pc_ref[...] = prw0_ref[0]

    pcv = pcv_ref[...]
    prw = prw_ref[...]
    n = pcv.shape[0]
    first = jnp.where(l == 0, n - N_META, 0)
    z = pcv[:, CONV_WIDTH:2 * CONV_WIDTH] * pcv[:, 2 * CONV_WIDTH:3 * CONV_WIDTH]
    row_z = lax.broadcasted_iota(jnp.int32, z.shape, 0)
    row_p = lax.broadcasted_iota(jnp.int32, prw.shape, 0)
    c2 = zc_ref[6:7, :]
    c1 = zc_ref[7:8, :]
    z1 = jnp.where(row_z == first, c1, pltpu.roll(z, 1, axis=0))
    z2 = jnp.where(row_z == first, c2, jnp.where(row_z == first + 1, c1, pltpu.roll(z, 2, axis=0)))
    prw_prev = jnp.where(row_p == first, pc_ref[7:8, :], pltpu.roll(prw, 1, axis=0))

    _, scan_in, tok_out = _mixer_core(pcv, prw, prw_prev, z1, z2, cw_ref[...], mu_ref[...], w0_ref[...],
                                      wup_ref[...], a0_ref[...], aup_ref[...], gup_ref[...], kkw_ref[...],
                                      kaw_ref[...], rkw_ref[...], ones_ref[...])
    for ref, val in zip(scan_refs, scan_in):
        ref[0] = val.T
    for ref, val in zip(tok_refs, tok_out):
        ref[...] = val
    tail = z[n - 8:n, :]
    zc_ref[...] = tail
    pc_ref[...] = prw[n - 8:n, :]
    convt_ref[0] = tail


def _mixer_row_kernel(pcv_ref, prw_ref, prwprev_ref, z1_ref, z2_ref, cw_ref, mu_ref, w0_ref, wup_ref, a0_ref,
                      aup_ref, gup_ref, kkw_ref, kaw_ref, rkw_ref, ones_ref, *rest):
    out_refs = rest[:N_SCAN_IN + N_TOK_OUT]
    z_ref = rest[N_SCAN_IN + N_TOK_OUT]
    z, scan_in, tok_out = _mixer_core(pcv_ref[...], prw_ref[...], prwprev_ref[...], z1_ref[...], z2_ref[...],
                                      cw_ref[...], mu_ref[...], w0_ref[...], wup_ref[...], a0_ref[...],
                                      aup_ref[...], gup_ref[...], kkw_ref[...], kaw_ref[...], rkw_ref[...],
                                      ones_ref[...])
    for ref, val in zip(out_refs, scan_in + tok_out):
        ref[...] = val
    z_ref[...] = z


def _mixer_param_specs():
    fixed = (lambda *_: (0, 0))
    shapes = [(3, CONV_WIDTH), (1, RW_PAD), (1, RWKV_WIDTH), (LORA_SLOT, RWKV_WIDTH), (1, RWKV_WIDTH),
              (LORA_SLOT, RWKV_WIDTH), (LORA_SLOT, RWKV_WIDTH), (1, RWKV_WIDTH), (1, RWKV_WIDTH), (1, RWKV_WIDTH),
              (RWKV_WIDTH, RWKV_WIDTH)]
    return [pl.BlockSpec(s, fixed) for s in shapes]


def _mixer_seq(pcv, prw, conv0, prw0, mix_params):
    blocks_per_seq = S_PITCH // T_BLK
    row = lambda b, l: (b * blocks_per_seq + _time_block(l), 0)
    chan = lambda b, l: (b, 0, _time_block(l))
    seq = lambda b, l: (b, 0, 0)
    return pl.pallas_call(
        _mixer_seq_kernel,
        grid=(BATCH, N_TIME_BLK),
        in_specs=[pl.BlockSpec((T_BLK, CV_COLS), row),
                  pl.BlockSpec((T_BLK, RW_PAD), row),
                  pl.BlockSpec((1, 8, CONV_WIDTH), seq),
                  pl.BlockSpec((1, 8, RW_PAD), seq)] + _mixer_param_specs(),
        out_specs=[pl.BlockSpec((1, RWKV_WIDTH, T_BLK), chan)] * N_SCAN_IN
                  + [pl.BlockSpec((T_BLK, RWKV_WIDTH), row)] * N_TOK_OUT
                  + [pl.BlockSpec((1, 8, CONV_WIDTH), seq)],
        out_shape=[jax.ShapeDtypeStruct((BATCH, RWKV_WIDTH, S_PITCH), F32)] * N_SCAN_IN
                  + [jax.ShapeDtypeStruct((T_PHYS, RWKV_WIDTH), F32)] * N_TOK_OUT
                  + [jax.ShapeDtypeStruct((BATCH, 8, CONV_WIDTH), F32)],
        scratch_shapes=[pltpu.VMEM((8, CONV_WIDTH), F32), pltpu.VMEM((8, RW_PAD), F32)],
        compiler_params=_params(("arbitrary", "arbitrary")),
        name="mixer_seq",
    )(pcv, prw, conv0, prw0, *mix_params)


def _mixer_rows(pcv, prw, prw_prev, z1, z2, mix_params):
    n = pcv.shape[0]
    return pl.pallas_call(
        _mixer_row_kernel,
        out_shape=[jax.ShapeDtypeStruct((n, RWKV_WIDTH), F32)] * (N_SCAN_IN + N_TOK_OUT + 1),
        compiler_params=_params(None),
        name="mixer_rows",
    )(pcv, prw, prw_prev, z1, z2, *mix_params)


V_ROWS = HEAD // 2
LANES = 128
PITCH_K = T_BLK + 8
PITCH_O = V_ROWS + 8
RETILE_UNROLL = 4


def _wkv_step(s_scr, r, w, kx, nkk, bb, v_row, o_row):
    for vl in range(V_ROWS):
        sv = s_scr[vl]
        sa = jnp.sum(sv * nkk, axis=0, keepdims=True)
        sn = sv * w + sa * bb + v_row(vl) * kx
        s_scr[vl] = sn
        o_row(vl, jnp.sum(sn * r, axis=0, keepdims=True))


def _wkv_seq_kernel(r_ref, w_ref, k_ref, nkk_ref, bb_ref, v_ref, s0_ref, o_ref, st_ref,
                    s_scr, rs, ws, ks, ns, bs, vs, os_):
    l = pl.program_id(0)

    @pl.when(l == 0)
    def _():
        s_scr[...] = s0_ref[...]
        os_[...] = jnp.zeros_like(os_)

    def retile(src_ref, dst_ref, idx, second_idx):
        lo = pl.multiple_of(idx * 8, 8)
        hi = pl.multiple_of(second_idx * 8, 8)
        tile = jnp.concatenate([x for b in range(BATCH)
                                for x in (src_ref[b, pl.ds(lo, 8), :], src_ref[b, pl.ds(hi, 8), :])], axis=0)
        dst_ref[pl.ds(pl.multiple_of(idx * PITCH_K, 8), T_BLK), :] = tile.T

    def retile_keys(n, carry):
        for src, dst in ((r_ref, rs), (w_ref, ws), (k_ref, ks), (nkk_ref, ns), (bb_ref, bs)):
            retile(src, dst, n, n)
        return carry

    def retile_vals(vl, carry):
        retile(v_ref, vs, vl, vl + V_ROWS)
        return carry

    lax.fori_loop(0, HEAD, retile_keys, 0, unroll=RETILE_UNROLL)
    lax.fori_loop(0, V_ROWS, retile_vals, 0, unroll=RETILE_UNROLL)

    def step(t, carry):
        def keys(ref):
            return jnp.concatenate([ref[pl.ds(g * 8 * PITCH_K + t, 8, stride=PITCH_K), :]
                                    for g in range(HEAD // 8)], axis=0)

        def o_row(vl, val):
            os_[pl.ds(t * PITCH_O + vl, 1), :] = val

        _wkv_step(s_scr, keys(rs), keys(ws), keys(ks), keys(ns), keys(bs),
                  lambda vl: vs[pl.ds(vl * PITCH_K + t, 1), :], o_row)
        return carry

    lax.fori_loop(jnp.where(l == 0, T_BLK - N_META, 0), T_BLK, step, 0)

    def write_out(vl, carry):
        tile = os_[pl.ds(vl, T_BLK, stride=PITCH_O), :].T
        for b in range(BATCH):
            for hf in range(2):
                src = (2 * b + hf) * 8
                o_ref[b, pl.ds(pl.multiple_of((hf * V_ROWS + vl) * 8, 8), 8), :] = tile[src:src + 8, :]
        return carry

    lax.fori_loop(0, V_ROWS, write_out, 0, unroll=RETILE_UNROLL)

    @pl.when(l == pl.num_programs(0) - 1)
    def _():
        st_ref[...] = s_scr[...]


def _wkv_seq(scan_in, s0):
    blk = pl.BlockSpec((BATCH, RWKV_WIDTH, T_BLK), lambda l: (0, 0, _time_block(l)))
    blk_in = pl.BlockSpec((BATCH, RWKV_WIDTH, T_BLK), lambda l: (0, 0, _time_block(l)),
                          pipeline_mode=pl.Buffered(1))
    st = pl.BlockSpec((V_ROWS, HEAD, LANES), lambda l: (0, 0, 0))
    retiled = pltpu.VMEM((HEAD * PITCH_K, LANES), F32)
    return pl.pallas_call(
        _wkv_seq_kernel,
        grid=(N_TIME_BLK,),
        in_specs=[blk_in] * N_SCAN_IN + [st],
        out_specs=[blk, st],
        out_shape=[jax.ShapeDtypeStruct((BATCH, RWKV_WIDTH, S_PITCH), F32),
                   jax.ShapeDtypeStruct((V_ROWS, HEAD, LANES), F32)],
        scratch_shapes=[pltpu.VMEM((V_ROWS, HEAD, LANES), F32)] + [retiled] * 5
                       + [pltpu.VMEM((V_ROWS * PITCH_K, LANES), F32), pltpu.VMEM((T_BLK * PITCH_O, LANES), F32)],
        compiler_params=_params(("arbitrary",)),
        name="wkv_seq",
    )(*scan_in, s0)


def _wkv_kernel(r_ref, w_ref, k_ref, nkk_ref, bb_ref, v_ref, s0_ref, o_ref, st_ref, s_scr):
    s_scr[...] = s0_ref[0]

    def o_row(vl, val):
        o_ref[0, pl.ds(vl, 1), :] = val

    _wkv_step(s_scr, r_ref[0], w_ref[0], k_ref[0], nkk_ref[0], bb_ref[0],
              lambda vl: v_ref[0, pl.ds(vl, 1), :], o_row)
    st_ref[0] = s_scr[...]


def _wkv_scan(r, w, k, nkk, bb, v, s0):
    groups = r.shape[0]
    key_spec = pl.BlockSpec((1, HEAD, LANES), lambda g: (g, 0, 0))
    val_spec = pl.BlockSpec((1, V_ROWS, LANES), lambda g: (g, 0, 0))
    st_spec = pl.BlockSpec((1, V_ROWS, HEAD, LANES), lambda g: (g, 0, 0, 0))
    return pl.pallas_call(
        _wkv_kernel,
        grid=(groups,),
        in_specs=[key_spec] * 5 + [val_spec, st_spec],
        out_specs=[val_spec, st_spec],
        out_shape=[jax.ShapeDtypeStruct((groups, V_ROWS, LANES), F32),
                   jax.ShapeDtypeStruct((groups, V_ROWS, HEAD, LANES), F32)],
        scratch_shapes=[pltpu.VMEM((V_ROWS, HEAD, LANES), F32)],
        compiler_params=_params(("parallel",)),
        name="wkv_scan",
    )(r, w, k, nkk, bb, v, s0)


def _to_scan_keys(x):
    g = x.shape[0] // BATCH
    x = x.reshape(g, BATCH, HEAD, HEADS).transpose(0, 2, 1, 3).reshape(g, HEAD, BATCH * HEADS)
    return jnp.concatenate([x, x], axis=-1)


def _to_scan_vals(x):
    g = x.shape[0] // BATCH
    return x.reshape(g, BATCH, 2, V_ROWS, HEADS).transpose(0, 3, 2, 1, 4).reshape(g, V_ROWS, LANES)


def _from_scan_vals(o):
    g = o.shape[0]
    return o.reshape(g, V_ROWS, 2, BATCH, HEADS).transpose(0, 3, 2, 1, 4).reshape(g * BATCH, RWKV_WIDTH)


def _state_to_scan(s):
    g = s.shape[0] // BATCH
    s = s.reshape(g, BATCH, HEADS, 2, V_ROWS, HEAD).transpose(0, 4, 5, 3, 1, 2)
    return s.reshape(g, V_ROWS, HEAD, LANES)


def _state_from_scan(s):
    g = s.shape[0]
    s = s.reshape(g, V_ROWS, HEAD, 2, BATCH, HEADS).transpose(0, 4, 5, 3, 1, 2)
    return s.reshape(g * BATCH, HEADS, HEAD, HEAD)


def _seq_state_from_scan(s):
    s = s.reshape(V_ROWS, HEAD, BATCH, 2, HEADS).transpose(2, 4, 3, 0, 1)
    return s.reshape(BATCH, HEADS, HEAD, HEAD)


def _mix_out_kernel(o_ref, g_ref, bv_ref, ycv_ref, h_ref, gng_ref, gnb_ref, ones_ref, wo1_ref, wo2_ref,
                    lg_ref, lb_ref, h1_ref, h1b_ref, *, o_channel_major):
    ones_hd = ones_ref[...]
    o = o_ref[0].T if o_channel_major else o_ref[...]
    d = o - _headsum(o, ones_hd) * (1.0 / HEAD)
    var = _headsum(d * d, ones_hd) * (1.0 / HEAD)
    gn = d * lax.rsqrt(var + GN_EPS) * gng_ref[...] + gnb_ref[...]
    y_rw = (gn + bv_ref[...]) * g_ref[...]
    mix = (jnp.dot(ycv_ref[...].astype(BF16), wo1_ref[...], preferred_element_type=F32)
           + jnp.dot(y_rw.astype(BF16), wo2_ref[...], preferred_element_type=F32))
    h1 = _layer_norm(ALPHA * h_ref[...] + mix, lg_ref[...], lb_ref[...])
    h1_ref[...] = h1
    h1b_ref[...] = h1.astype(BF16)


def _mix_out_call(kernel, grid, o_spec, tok_map, out_map, n_out, tb, args):
    fixed = lambda *_: (0, 0)
    tok512 = pl.BlockSpec((tb, RWKV_WIDTH), tok_map)
    return pl.pallas_call(
        kernel,
        grid=grid,
        in_specs=[o_spec, tok512, tok512, tok512, pl.BlockSpec((tb, D_MODEL), tok_map),
                  pl.BlockSpec((1, RWKV_WIDTH), fixed), pl.BlockSpec((1, RWKV_WIDTH), fixed),
                  pl.BlockSpec((RWKV_WIDTH, RWKV_WIDTH), fixed),
                  pl.BlockSpec((CONV_WIDTH, D_MODEL), fixed), pl.BlockSpec((RWKV_WIDTH, D_MODEL), fixed),
                  pl.BlockSpec((1, D_MODEL), fixed), pl.BlockSpec((1, D_MODEL), fixed)],
        out_specs=[pl.BlockSpec((tb, D_MODEL), out_map), pl.BlockSpec((tb, D_MODEL), out_map)],
        out_shape=[jax.ShapeDtypeStruct((n_out, D_MODEL), F32), jax.ShapeDtypeStruct((n_out, D_MODEL), BF16)],
        compiler_params=_params(("parallel",) * len(grid)),
        name="mix_out",
    )(*args)


def _mix_out_seq(o, g, bv, ycv, h, out_params):
    per_seq = S_PITCH // TB_OUT
    real = SEQ // TB_OUT
    return _mix_out_call(functools.partial(_mix_out_kernel, o_channel_major=True), (BATCH, real),
                         pl.BlockSpec((1, RWKV_WIDTH, TB_OUT), lambda b, j: (b, 0, j)),
                         lambda b, j: (b * per_seq + j, 0), lambda b, j: (b * real + j, 0),
                         BATCH * SEQ, TB_OUT, (o, g, bv, ycv, h, *out_params))


def _mix_out_rows(o, g, bv, ycv, h, h_block, out_params):
    n = o.shape[0]
    tok = lambda i: (0, 0)
    fixed = lambda *_: (0, 0)
    tok512 = pl.BlockSpec((n, RWKV_WIDTH), tok)
    return pl.pallas_call(
        functools.partial(_mix_out_kernel, o_channel_major=False),
        grid=(1,),
        in_specs=[tok512, tok512, tok512, tok512, pl.BlockSpec((n, D_MODEL), lambda i: (h_block, 0)),
                  pl.BlockSpec((1, RWKV_WIDTH), fixed), pl.BlockSpec((1, RWKV_WIDTH), fixed),
                  pl.BlockSpec((RWKV_WIDTH, RWKV_WIDTH), fixed),
                  pl.BlockSpec((CONV_WIDTH, D_MODEL), fixed), pl.BlockSpec((RWKV_WIDTH, D_MODEL), fixed),
                  pl.BlockSpec((1, D_MODEL), fixed), pl.BlockSpec((1, D_MODEL), fixed)],
        out_specs=[pl.BlockSpec((n, D_MODEL), tok), pl.BlockSpec((n, D_MODEL), tok)],
        out_shape=[jax.ShapeDtypeStruct((n, D_MODEL), F32), jax.ShapeDtypeStruct((n, D_MODEL), BF16)],
        compiler_params=_params(("arbitrary",)),
        name="mix_out_rows",
    )(o, g, bv, ycv, h, *out_params)


CAND_ROWS = TOPK + 7 * 8 + 8


def _extract_top(s, ids, count):
    rank = jnp.full(s.shape, NOT_RANKED, F32)
    big = jnp.float32(NOT_RANKED)
    vals, picks = [], []
    for p in range(count):
        m = jnp.max(s, axis=0, keepdims=True)
        pick = jnp.min(jnp.where(s == m, ids, big), axis=0, keepdims=True)
        hit = ids == pick
        rank = jnp.where(hit, jnp.float32(p), rank)
        s = jnp.where(hit, -jnp.inf, s)
        vals.append(m)
        picks.append(pick)
    return vals, picks, rank


def _peer_route_kernel(x_ref, wqt_ref, keys_ref, n1_ref, e1_ref, r2_ref, e2_ref):
    xb = x_ref[...]
    qt = lax.dot_general(wqt_ref[...], xb, (((1,), (1,)), ((), ())), preferred_element_type=F32)
    lanes = xb.shape[0]
    key_ids = lax.broadcasted_iota(jnp.int32, (N_KEYS, lanes), 0).astype(F32)
    row16 = lax.broadcasted_iota(jnp.int32, (TOPK, lanes), 0).astype(F32)
    row8 = lax.broadcasted_iota(jnp.int32, (8, lanes), 0).astype(F32)
    cand_ids = jnp.concatenate([row16] + [row8 + float(TOPK * p) for p in range(1, 8)]
                               + [(row8 + 8.0) * float(TOPK)], axis=0)
    for h in range(PEER_HEADS):
        s1 = jnp.dot(keys_ref[2 * h], qt[(2 * h) * HALF:(2 * h + 1) * HALF, :].astype(BF16),
                     preferred_element_type=F32)
        s2 = jnp.dot(keys_ref[2 * h + 1], qt[(2 * h + 1) * HALF:(2 * h + 2) * HALF, :].astype(BF16),
                     preferred_element_type=F32)
        a_vals, _, rank1 = _extract_top(s1, key_ids, TOPK)
        b_vals, _, rank2 = _extract_top(s2, key_ids, TOPK)
        b = jnp.zeros((TOPK, lanes), F32)
        a_hi = jnp.zeros((8, lanes), F32)
        for p in range(TOPK):
            b = jnp.where(row16 == float(p), b_vals[p], b)
        for p in range(8):
            a_hi = jnp.where(row8 == float(p), a_vals[8 + p], a_hi)
        cand = jnp.concatenate([a_vals[0] + b] + [a_vals[p] + b[0:8, :] for p in range(1, 8)]
                               + [a_hi + b_vals[0]], axis=0)
        _, _, crank = _extract_top(cand, cand_ids, TOPK)
        sel = crank < float(TOPK)
        top = a_vals[0] + b_vals[0]
        z = jnp.sum(jnp.where(sel, jnp.exp(cand - top), 0.0), axis=0, keepdims=True)
        self32 = jnp.where(sel, 1.0, 0.0)
        counts = [jnp.sum(self32[0:TOPK, :], axis=0, keepdims=True)]
        counts += [jnp.sum(self32[TOPK + 8 * (p - 1):TOPK + 8 * p, :], axis=0, keepdims=True) for p in range(1, 8)]
        tail = self32[TOPK + 56:TOPK + 64, :]
        counts += [jnp.sum(jnp.where(row8 == float(p), tail, 0.0), axis=0, keepdims=True) for p in range(8)]
        n1 = jnp.zeros((N_KEYS, lanes), F32)
        for p in range(TOPK):
            n1 = jnp.where(rank1 == float(p), counts[p], n1)
        n1_ref[h] = n1
        e1_ref[h] = jnp.exp(s1 - a_vals[0])
        r2_ref[h] = rank2.astype(BF16)
        e2_ref[h] = (jnp.exp(s2 - b_vals[0]) * (0.5 / z)).astype(BF16)


def _peer_route(h1b, wqt, keys):
    n = h1b.shape[0]
    maps = pl.BlockSpec((PEER_HEADS, N_KEYS, TB_ROUTE), lambda i: (0, 0, i))
    f32_map = jax.ShapeDtypeStruct((PEER_HEADS, N_KEYS, n), F32)
    bf16_map = jax.ShapeDtypeStruct((PEER_HEADS, N_KEYS, n), BF16)
    return pl.pallas_call(
        _peer_route_kernel,
        grid=(n // TB_ROUTE,),
        in_specs=[pl.BlockSpec((TB_ROUTE, D_MODEL), lambda i: (i, 0)),
                  pl.BlockSpec((2 * PEER_HEADS * HALF, D_MODEL), lambda i: (0, 0)),
                  pl.BlockSpec((2 * PEER_HEADS, N_KEYS, HALF), lambda i: (0, 0, 0))],
        out_specs=[maps] * 4,
        out_shape=[f32_map, f32_map, bf16_map, bf16_map],
        compiler_params=_params(("parallel",)),
        name="peer_route",
    )(h1b, wqt, keys)


SLAB = 16


def _peer_dense_kernel(xb_ref, x_ref, u_ref, vt_ref, n1_ref, e1_ref, r2_ref, e2_ref, lg_ref, lb_ref,
                       y_ref, acc_ref, ht_ref, act_ref):
    c = pl.program_id(1)

    @pl.when(c == 0)
    def _():
        acc_ref[...] = jnp.zeros_like(acc_ref)

    lanes = ht_ref.shape[1]
    zero = jnp.zeros((SLAB, lanes), BF16)
    xb = xb_ref[...]
    for sub in range(I_PER_CHUNK // I_PER_SUB):
        sub_rows = slice(sub * E_SUB, (sub + 1) * E_SUB)
        ht_ref[sub_rows, :] = lax.dot_general(u_ref[sub_rows, :], xb, (((1,), (1,)), ((), ())),
                                              preferred_element_type=F32)
    for sub in range(I_PER_CHUNK // I_PER_SUB):
        sub_rows = slice(sub * E_SUB, (sub + 1) * E_SUB)
        for ii in range(sub * I_PER_SUB, (sub + 1) * I_PER_SUB):
            n1 = [jnp.broadcast_to(n1_ref[h, 0, ii:ii + 1, :], (SLAB, lanes)).astype(BF16)
                  for h in range(PEER_HEADS)]
            e1 = [jnp.broadcast_to(e1_ref[h, 0, ii:ii + 1, :], (SLAB, lanes)).astype(BF16)
                  for h in range(PEER_HEADS)]
            for js in range(N_KEYS // SLAB):
                rows = slice(js * SLAB, (js + 1) * SLAB)
                gate = jnp.where(r2_ref[0, rows, :] < n1[0], e2_ref[0, rows, :], zero) * e1[0]
                for h in range(1, PEER_HEADS):
                    gate = gate + jnp.where(r2_ref[h, rows, :] < n1[h], e2_ref[h, rows, :], zero) * e1[h]
                out_rows = slice(ii * N_KEYS + js * SLAB, ii * N_KEYS + (js + 1) * SLAB)
                ht = ht_ref[out_rows, :]
                gelu2 = ht * (1.0 + lax.erf(ht * SQRT_HALF))
                act_ref[out_rows, :] = gelu2.astype(BF16) * gate
        acc_ref[...] += jnp.dot(vt_ref[:, sub_rows], act_ref[sub_rows, :], preferred_element_type=F32)

    @pl.when(c == pl.num_programs(1) - 1)
    def _():
        y_ref[...] = _layer_norm(ALPHA * x_ref[...] + acc_ref[...].T, lg_ref[...], lb_ref[...])


def _peer_dense(h1b, h1, u, vt, n1, e1, r2, e2, lg, lb):
    n = h1b.shape[0]
    n_chunks = N_EXPERTS // E_CHUNK
    tok = pl.BlockSpec((TB_PEER, D_MODEL), lambda i, c: (i, 0))
    by_first = pl.BlockSpec((PEER_HEADS, 1, I_PER_CHUNK, TB_PEER), lambda i, c: (0, c, 0, i))
    by_second = pl.BlockSpec((PEER_HEADS, N_KEYS, TB_PEER), lambda i, c: (0, 0, i))
    vec = pl.BlockSpec((1, D_MODEL), lambda i, c: (0, 0))
    return pl.pallas_call(
        _peer_dense_kernel,
        grid=(n // TB_PEER, n_chunks),
        in_specs=[tok, tok,
                  pl.BlockSpec((E_CHUNK, D_MODEL), lambda i, c: (c, 0)),
                  pl.BlockSpec((D_MODEL, E_CHUNK), lambda i, c: (0, c)),
                  by_first, by_first, by_second, by_second, vec, vec],
        out_specs=tok,
        out_shape=jax.ShapeDtypeStruct((n, D_MODEL), F32),
        scratch_shapes=[pltpu.VMEM((D_MODEL, TB_PEER), F32), pltpu.VMEM((E_CHUNK, TB_PEER), F32),
                        pltpu.VMEM((E_CHUNK, TB_PEER), BF16)],
        compiler_params=_params(("parallel", "arbitrary")),
        name="peer_dense",
    )(h1b, h1, u, vt, n1, e1, r2, e2, lg, lb)


def _pad_cols(x, width):
    return jnp.pad(x, ((0, 0), (0, width - x.shape[1])))


def _pad_rows(x, height):
    return jnp.pad(x, ((0, height - x.shape[0]), (0, 0)))


def kernel(x_prompt, x_sample, state_conv, state_shift, state_wkv, meta_tokens, ln_in_g, ln_in_b, w_in, conv_w, mu, w0, w_up, a0, a_up, g_up, k_k, k_a, r_k, gn_g, gn_b, w_o, ln1_g, ln1_b, w_q, sub_keys, peer_u, peer_v, ln2_g, ln2_b):
    row = lambda x: x.reshape(1, -1)
    chan = np.arange(RWKV_WIDTH)
    perm = (chan % HEADS) * HEAD + chan // HEADS
    pc = lambda x: x[..., perm]
    w_in0 = w_in[0]
    wcv = w_in0[:, :CV_COLS].astype(BF16)
    rw0 = CV_COLS
    lo0 = rw0 + 3 * RWKV_WIDTH

    def rw_layout(x):
        groups = [pc(x[:, rw0 + i * RWKV_WIDTH:rw0 + (i + 1) * RWKV_WIDTH]) for i in range(3)]
        return jnp.concatenate(groups + [_pad_cols(x[:, lo0:lo0 + W_LORA], LORA_SLOT),
                                         _pad_cols(x[:, lo0 + W_LORA:lo0 + W_LORA + A_LORA], LORA_SLOT),
                                         _pad_cols(x[:, lo0 + W_LORA + A_LORA:], LORA_SLOT)], axis=1)

    wrw = rw_layout(w_in0).astype(BF16)
    mu_p = rw_layout(jnp.pad(mu, ((0, 0), (CV_COLS, 0))))
    ones_hd = jnp.asarray((chan[:, None] % HEADS) == (chan[None, :] % HEADS), BF16)
    lora = lambda w: _pad_rows(pc(w[0]), LORA_SLOT).astype(BF16)
    mix_params = (conv_w[0], mu_p, pc(w0), lora(w_up), pc(a0), lora(a_up), lora(g_up),
                  pc(k_k), pc(k_a), pc(r_k.reshape(1, RWKV_WIDTH)), ones_hd)
    out_params = (pc(gn_g), pc(gn_b), ones_hd, w_o[0, :CONV_WIDTH].astype(BF16),
                  w_o[0, CONV_WIDTH:][perm].astype(BF16), ln1_g, ln1_b)

    dt = x_prompt.dtype
    seq_pad = jnp.zeros((BATCH, S_PITCH - SEQ - N_META, D_MODEL), dt)
    meta = jnp.broadcast_to(meta_tokens.astype(dt)[None], (BATCH, N_META, D_MODEL))
    x_all = jnp.concatenate([jnp.concatenate([x_prompt, seq_pad, meta], axis=1).reshape(T_SEQS, D_MODEL),
                             x_sample.reshape(DEC_BATCH, D_MODEL),
                             jnp.zeros((T_PHYS - T_SEQS - DEC_BATCH, D_MODEL), dt)], axis=0)
    h, pcv, prw = _ln_proj(x_all, row(ln_in_g), row(ln_in_b), wcv, wrw)
    prev_rw_s = _shift_proj(state_shift[0], wrw)

    outs_p = _mixer_seq(pcv, prw, jnp.zeros((BATCH, 8, CONV_WIDTH), F32), jnp.zeros((BATCH, 8, RW_PAD), F32),
                        mix_params)
    scan_p = outs_p[:N_SCAN_IN]
    ycv_p, g_p, bv_p = outs_p[N_SCAN_IN:N_SCAN_IN + N_TOK_OUT]
    conv_p = outs_p[N_SCAN_IN + N_TOK_OUT][:, 6:8]
    o_p, st_p = _wkv_seq(scan_p, jnp.zeros((V_ROWS, HEAD, LANES), F32))
    wkv_p = _seq_state_from_scan(st_p)
    h1_p, h1b_p = _mix_out_seq(o_p, g_p, bv_p, ycv_p, h, out_params)

    s_rows = slice(T_SEQS, T_SEQS + DEC_BATCH)
    outs_s = _mixer_rows(pcv[s_rows], prw[s_rows], prev_rw_s, state_conv[0, :, 1], state_conv[0, :, 0], mix_params)
    r_s, w_s, k_s, nkk_s, bb_s, v_s, ycv_s, g_s, bv_s, z_s = outs_s
    o_s, st_s = _wkv_scan(_to_scan_keys(r_s), _to_scan_keys(w_s), _to_scan_keys(k_s), _to_scan_keys(nkk_s),
                          _to_scan_keys(bb_s), _to_scan_vals(v_s), _state_to_scan(state_wkv[0]))
    wkv_s = _state_from_scan(st_s)
    h1_s, h1b_s = _mix_out_rows(_from_scan_vals(o_s), g_s, bv_s, ycv_s, h, T_SEQS // DEC_BATCH, out_params)

    pad = T_PEER - T_REAL
    h1 = jnp.concatenate([h1_p, h1_s, jnp.zeros((pad, D_MODEL), F32)], axis=0)
    h1b = jnp.concatenate([h1b_p, h1b_s, jnp.zeros((pad, D_MODEL), BF16)], axis=0)

    wqt = w_q[0].T.astype(BF16)
    keys = sub_keys[0].reshape(2 * PEER_HEADS, N_KEYS, HALF).astype(BF16)
    n1, e1, r2, e2 = _peer_route(h1b, wqt, keys)
    by_first = lambda x: x.reshape(PEER_HEADS, N_KEYS // I_PER_CHUNK, I_PER_CHUNK, T_PEER)
    y = _peer_dense(h1b, h1, peer_u[0].astype(BF16), peer_v[0].T.astype(BF16), by_first(n1), by_first(e1), r2, e2,
                    row(ln2_g), row(ln2_b))

    y_prompt = y[:BATCH * SEQ].reshape(BATCH, SEQ, D_MODEL)
    y_sample = y[BATCH * SEQ:T_REAL].reshape(DEC_BATCH, 1, D_MODEL)
    shift_p = h[:T_SEQS].reshape(BATCH, S_PITCH, D_MODEL)[:, SEQ - 1]
    conv_s = jnp.stack([state_conv[0, :, 1], z_s], axis=1)
    return (y_prompt, y_sample, conv_p[None], shift_p[None], wkv_p[None],
            conv_s[None], h[s_rows][None], wkv_s[None])
```

```python
import functools

import numpy as np
import jax
import jax.numpy as jnp
from jax import lax
from jax.experimental import pallas as pl
from jax.experimental.pallas import tpu as pltpu

F32 = jnp.float32
BF16 = jnp.bfloat16

D_MODEL = 1024
N_META = 16
BATCH = 8
SEQ = 2048
DEC_BATCH = 128
T_BLK = 128
S_PITCH = 2304
N_TIME_BLK = SEQ // T_BLK + 1
META_BLK = S_PITCH // T_BLK - 1
T_SEQS = BATCH * S_PITCH
T_PHYS = T_SEQS + 512
T_REAL = BATCH * SEQ + DEC_BATCH
T_PEER = 16896
CONV_WIDTH = 512
RWKV_WIDTH = 512
HEAD = 64
HEADS = 8
W_LORA = 32
A_LORA = 32
G_LORA = 96
CV_COLS = 3 * CONV_WIDTH
LORA_SLOT = 128
RW_PAD = 3 * RWKV_WIDTH + 3 * LORA_SLOT
PEER_HEADS = 8
N_KEYS = 128
N_EXPERTS = N_KEYS * N_KEYS
TOPK = 16
HALF = 128
LN_EPS = 1e-5
GN_EPS = HEAD * 1e-5
ALPHA = 2.0 ** 0.25
SQRT_HALF = float(np.sqrt(0.5))
NOT_RANKED = 256.0
VMEM_LIMIT_BYTES = 56 * 1024 * 1024

TB_PROJ = 256
TB_OUT = 256
TB_ROUTE = 256
TB_PEER = 512
I_PER_CHUNK = 16
E_CHUNK = I_PER_CHUNK * N_KEYS
I_PER_SUB = 4
E_SUB = I_PER_SUB * N_KEYS


def _params(sem):
    return pltpu.CompilerParams(dimension_semantics=sem, vmem_limit_bytes=VMEM_LIMIT_BYTES)


def _layer_norm(x, g, b):
    m = jnp.mean(x, axis=-1, keepdims=True)
    xc = x - m
    var = jnp.mean(xc * xc, axis=-1, keepdims=True)
    return xc * lax.rsqrt(var + LN_EPS) * g + b


def _headsum(x, ones_hd):
    hi = x.astype(BF16)
    lo = (x - hi.astype(F32)).astype(BF16)
    return (jnp.dot(hi, ones_hd, preferred_element_type=F32)
            + jnp.dot(lo, ones_hd, preferred_element_type=F32))


def _time_block(l):
    return jnp.where(l == 0, META_BLK, l - 1)


def _ln_proj_kernel(x_ref, g_ref, b_ref, wcv_ref, wrw_ref, h_ref, pcv_ref, prw_ref):
    h = _layer_norm(x_ref[...], g_ref[...], b_ref[...])
    h_ref[...] = h
    hb = h.astype(BF16)
    pcv_ref[...] = jnp.dot(hb, wcv_ref[...], preferred_element_type=F32)
    prw_ref[...] = jnp.dot(hb, wrw_ref[...], preferred_element_type=F32)


def _ln_proj(x_all, g, b, wcv, wrw):
    n = x_all.shape[0]
    row = lambda i: (i, 0)
    fixed = lambda i: (0, 0)
    return pl.pallas_call(
        _ln_proj_kernel,
        grid=(n // TB_PROJ,),
        in_specs=[pl.BlockSpec((TB_PROJ, D_MODEL), row),
                  pl.BlockSpec((1, D_MODEL), fixed),
                  pl.BlockSpec((1, D_MODEL), fixed),
                  pl.BlockSpec((D_MODEL, CV_COLS), fixed),
                  pl.BlockSpec((D_MODEL, RW_PAD), fixed)],
        out_specs=[pl.BlockSpec((TB_PROJ, D_MODEL), row),
                   pl.BlockSpec((TB_PROJ, CV_COLS), row),
                   pl.BlockSpec((TB_PROJ, RW_PAD), row)],
        out_shape=[jax.ShapeDtypeStruct((n, D_MODEL), F32),
                   jax.ShapeDtypeStruct((n, CV_COLS), F32),
                   jax.ShapeDtypeStruct((n, RW_PAD), F32)],
        compiler_params=_params(("parallel",)),
        name="ln_proj",
    )(x_all, g, b, wcv, wrw)


def _shift_proj_kernel(x_ref, wrw_ref, prw_ref):
    prw_ref[...] = jnp.dot(x_ref[...].astype(BF16), wrw_ref[...], preferred_element_type=F32)


def _shift_proj(x, wrw):
    n = x.shape[0]
    return pl.pallas_call(
        _shift_proj_kernel,
        out_shape=jax.ShapeDtypeStruct((n, RW_PAD), F32),
        compiler_params=_params(None),
        name="shift_proj",
    )(x, wrw)


N_SCAN_IN = 6
N_TOK_OUT = 3


def _mixer_core(pcv, prw, prw_prev, z1, z2, cw, mu, w0, wup, a0, aup, gup, kkw, kaw, rkw, ones_hd):
    bg = pcv[:, 0:CONV_WIDTH]
    z = pcv[:, CONV_WIDTH:2 * CONV_WIDTH] * pcv[:, 2 * CONV_WIDTH:3 * CONV_WIDTH]
    conv = cw[0:1, :] * z2 + cw[1:2, :] * z1 + cw[2:3, :] * z
    y_cv = bg * conv

    m = prw + (prw_prev - prw) * mu
    r = m[:, 0:RWKV_WIDTH]
    k = m[:, RWKV_WIDTH:2 * RWKV_WIDTH]
    v = m[:, 2 * RWKV_WIDTH:3 * RWKV_WIDTH]
    base = 3 * RWKV_WIDTH
    wd = m[:, base:base + LORA_SLOT]
    ad = m[:, base + LORA_SLOT:base + 2 * LORA_SLOT]
    gd = m[:, base + 2 * LORA_SLOT:base + 3 * LORA_SLOT]

    xw = -(w0 + jnp.dot(jnp.tanh(wd).astype(BF16), wup, preferred_element_type=F32))
    softplus = jnp.maximum(xw, 0.0) + jnp.log1p(jnp.exp(-jnp.abs(xw)))
    w_log = -softplus - 0.5
    decay = jnp.exp(-jnp.exp(w_log))
    a = jax.nn.sigmoid(a0 + jnp.dot(ad.astype(BF16), aup, preferred_element_type=F32))
    g = jnp.dot(jax.nn.sigmoid(gd).astype(BF16), gup, preferred_element_type=F32)

    kk = k * kkw
    norm = jnp.sqrt(_headsum(kk * kk, ones_hd))
    kk = kk / jnp.maximum(norm, 1e-12)
    kx = k * (1.0 + (a - 1.0) * kaw)
    bonus_v = _headsum(r * kx * rkw, ones_hd) * v
    return z, (r, decay, kx, -kk, kk * a, v), (y_cv, g, bonus_v)


def _mixer_seq_kernel(pcv_ref, prw_ref, conv0_ref, prw0_ref, cw_ref, mu_ref, w0_ref, wup_ref, a0_ref, aup_ref,
                      gup_ref, kkw_ref, kaw_ref, rkw_ref, ones_ref, *rest):
    scan_refs = rest[:N_SCAN_IN]
    tok_refs = rest[N_SCAN_IN:N_SCAN_IN + N_TOK_OUT]
    convt_ref = rest[N_SCAN_IN + N_TOK_OUT]
    zc_ref, pc_ref = rest[N_SCAN_IN + N_TOK_OUT + 1:]
    l = pl.program_id(1)

    @pl.when(l == 0)
    def _():
        zc_ref[...] = conv0_ref[0]
        pc_ref[...] = prw0_ref[0]

    pcv = pcv_ref[...]
    prw = prw_ref[...]
    n = pcv.shape[0]
    first = jnp.where(l == 0, n - N_META, 0)
    z = pcv[:, CONV_WIDTH:2 * CONV_WIDTH] * pcv[:, 2 * CONV_WIDTH:3 * CONV_WIDTH]
    row_z = lax.broadcasted_iota(jnp.int32, z.shape, 0)
    row_p = lax.broadcasted_iota(jnp.int32, prw.shape, 0)
    c2 = zc_ref[6:7, :]
    c1 = zc_ref[7:8, :]
    z1 = jnp.where(row_z == first, c1, pltpu.roll(z, 1, axis=0))
    z2 = jnp.where(row_z == first, c2, jnp.where(row_z == first + 1, c1, pltpu.roll(z, 2, axis=0)))
    prw_prev = jnp.where(row_p == first, pc_ref[7:8, :], pltpu.roll(prw, 1, axis=0))

    _, scan_in, tok_out = _mixer_core(pcv, prw, prw_prev, z1, z2, cw_ref[...], mu_ref[...], w0_ref[...],
                                      wup_ref[...], a0_ref[...], aup_ref[...], gup_ref[...], kkw_ref[...],
                                      kaw_ref[...], rkw_ref[...], ones_ref[...])
    for ref, val in zip(scan_refs, scan_in):
        ref[0] = val.T
    for ref, val in zip(tok_refs, tok_out):
        ref[...] = val
    tail = z[n - 8:n, :]
    zc_ref[...] = tail
    pc_ref[...] = prw[n - 8:n, :]
    convt_ref[0] = tail


def _mixer_row_kernel(pcv_ref, prw_ref, prwprev_ref, z1_ref, z2_ref, cw_ref, mu_ref, w0_ref, wup_ref, a0_ref,
                      aup_ref, gup_ref, kkw_ref, kaw_ref, rkw_ref, ones_ref, *rest):
    out_refs = rest[:N_SCAN_IN + N_TOK_OUT]
    z_ref = rest[N_SCAN_IN + N_TOK_OUT]
    z, scan_in, tok_out = _mixer_core(pcv_ref[...], prw_ref[...], prwprev_ref[...], z1_ref[...], z2_ref[...],
                                      cw_ref[...], mu_ref[...], w0_ref[...], wup_ref[...], a0_ref[...],
                                      aup_ref[...], gup_ref[...], kkw_ref[...], kaw_ref[...], rkw_ref[...],
                                      ones_ref[...])
    for ref, val in zip(out_refs, scan_in + tok_out):
        ref[...] = val
    z_ref[...] = z


def _mixer_param_specs():
    fixed = (lambda *_: (0, 0))
    shapes = [(3, CONV_WIDTH), (1, RW_PAD), (1, RWKV_WIDTH), (LORA_SLOT, RWKV_WIDTH), (1, RWKV_WIDTH),
              (LORA_SLOT, RWKV_WIDTH), (LORA_SLOT, RWKV_WIDTH), (1, RWKV_WIDTH), (1, RWKV_WIDTH), (1, RWKV_WIDTH),
              (RWKV_WIDTH, RWKV_WIDTH)]
    return [pl.BlockSpec(s, fixed) for s in shapes]


def _mixer_seq(pcv, prw, conv0, prw0, mix_params):
    blocks_per_seq = S_PITCH // T_BLK
    row = lambda b, l: (b * blocks_per_seq + _time_block(l), 0)
    chan = lambda b, l: (b, 0, _time_block(l))
    seq = lambda b, l: (b, 0, 0)
    return pl.pallas_call(
        _mixer_seq_kernel,
        grid=(BATCH, N_TIME_BLK),
        in_specs=[pl.BlockSpec((T_BLK, CV_COLS), row),
                  pl.BlockSpec((T_BLK, RW_PAD), row),
                  pl.BlockSpec((1, 8, CONV_WIDTH), seq),
                  pl.BlockSpec((1, 8, RW_PAD), seq)] + _mixer_param_specs(),
        out_specs=[pl.BlockSpec((1, RWKV_WIDTH, T_BLK), chan)] * N_SCAN_IN
                  + [pl.BlockSpec((T_BLK, RWKV_WIDTH), row)] * N_TOK_OUT
                  + [pl.BlockSpec((1, 8, CONV_WIDTH), seq)],
        out_shape=[jax.ShapeDtypeStruct((BATCH, RWKV_WIDTH, S_PITCH), F32)] * N_SCAN_IN
                  + [jax.ShapeDtypeStruct((T_PHYS, RWKV_WIDTH), F32)] * N_TOK_OUT
                  + [jax.ShapeDtypeStruct((BATCH, 8, CONV_WIDTH), F32)],
        scratch_shapes=[pltpu.VMEM((8, CONV_WIDTH), F32), pltpu.VMEM((8, RW_PAD), F32)],
        compiler_params=_params(("arbitrary", "arbitrary")),
        name="mixer_seq",
    )(pcv, prw, conv0, prw0, *mix_params)


def _mixer_rows(pcv, prw, prw_prev, z1, z2, mix_params):
    n = pcv.shape[0]
    return pl.pallas_call(
        _mixer_row_kernel,
        out_shape=[jax.ShapeDtypeStruct((n, RWKV_WIDTH), F32)] * (N_SCAN_IN + N_TOK_OUT + 1),
        compiler_params=_params(None),
        name="mixer_rows",
    )(pcv, prw, prw_prev, z1, z2, *mix_params)


V_ROWS = HEAD // 2
LANES = 128
PITCH_K = T_BLK + 8
PITCH_O = V_ROWS + 8
RETILE_UNROLL = 4


def _wkv_step(s_scr, r, w, kx, nkk, bb, v_row, o_row):
    for vl in range(V_ROWS):
        sv = s_scr[vl]
        sa = jnp.sum(sv * nkk, axis=0, keepdims=True)
        sn = sv * w + sa * bb + v_row(vl) * kx
        s_scr[vl] = sn
        o_row(vl, jnp.sum(sn * r, axis=0, keepdims=True))


def _wkv_seq_kernel(r_ref, w_ref, k_ref, nkk_ref, bb_ref, v_ref, s0_ref, o_ref, st_ref,
                    s_scr, rs, ws, ks, ns, bs, vs, os_):
    l = pl.program_id(0)

    @pl.when(l == 0)
    def _():
        s_scr[...] = s0_ref[...]
        os_[...] = jnp.zeros_like(os_)

    def retile(src_ref, dst_ref, idx, second_idx):
        lo = pl.multiple_of(idx * 8, 8)
        hi = pl.multiple_of(second_idx * 8, 8)
        tile = jnp.concatenate([x for b in range(BATCH)
                                for x in (src_ref[b, pl.ds(lo, 8), :], src_ref[b, pl.ds(hi, 8), :])], axis=0)
        dst_ref[pl.ds(pl.multiple_of(idx * PITCH_K, 8), T_BLK), :] = tile.T

    def retile_keys(n, carry):
        for src, dst in ((r_ref, rs), (w_ref, ws), (k_ref, ks), (nkk_ref, ns), (bb_ref, bs)):
            retile(src, dst, n, n)
        return carry

    def retile_vals(vl, carry):
        retile(v_ref, vs, vl, vl + V_ROWS)
        return carry

    lax.fori_loop(0, HEAD, retile_keys, 0, unroll=RETILE_UNROLL)
    lax.fori_loop(0, V_ROWS, retile_vals, 0, unroll=RETILE_UNROLL)

    def step(t, carry):
        def keys(ref):
            return jnp.concatenate([ref[pl.ds(g * 8 * PITCH_K + t, 8, stride=PITCH_K), :]
                                    for g in range(HEAD // 8)], axis=0)

        def o_row(vl, val):
            os_[pl.ds(t * PITCH_O + vl, 1), :] = val

        _wkv_step(s_scr, keys(rs), keys(ws), keys(ks), keys(ns), keys(bs),
                  lambda vl: vs[pl.ds(vl * PITCH_K + t, 1), :], o_row)
        return carry

    lax.fori_loop(jnp.where(l == 0, T_BLK - N_META, 0), T_BLK, step, 0)

    def write_out(vl, carry):
        tile = os_[pl.ds(vl, T_BLK, stride=PITCH_O), :].T
        for b in range(BATCH):
            for hf in range(2):
                src = (2 * b + hf) * 8
                o_ref[b, pl.ds(pl.multiple_of((hf * V_ROWS + vl) * 8, 8), 8), :] = tile[src:src + 8, :]
        return carry

    lax.fori_loop(0, V_ROWS, write_out, 0, unroll=RETILE_UNROLL)

    @pl.when(l == pl.num_programs(0) - 1)
    def _():
        st_ref[...] = s_scr[...]


def _wkv_seq(scan_in, s0):
    blk = pl.BlockSpec((BATCH, RWKV_WIDTH, T_BLK), lambda l: (0, 0, _time_block(l)))
    blk_in = pl.BlockSpec((BATCH, RWKV_WIDTH, T_BLK), lambda l: (0, 0, _time_block(l)),
                          pipeline_mode=pl.Buffered(1))
    st = pl.BlockSpec((V_ROWS, HEAD, LANES), lambda l: (0, 0, 0))
    retiled = pltpu.VMEM((HEAD * PITCH_K, LANES), F32)
    return pl.pallas_call(
        _wkv_seq_kernel,
        grid=(N_TIME_BLK,),
        in_specs=[blk_in] * N_SCAN_IN + [st],
        out_specs=[blk, st],
        out_shape=[jax.ShapeDtypeStruct((BATCH, RWKV_WIDTH, S_PITCH), F32),
                   jax.ShapeDtypeStruct((V_ROWS, HEAD, LANES), F32)],
        scratch_shapes=[pltpu.VMEM((V_ROWS, HEAD, LANES), F32)] + [retiled] * 5
                       + [pltpu.VMEM((V_ROWS * PITCH_K, LANES), F32), pltpu.VMEM((T_BLK * PITCH_O, LANES), F32)],
        compiler_params=_params(("arbitrary",)),
        name="wkv_seq",
    )(*scan_in, s0)


def _wkv_kernel(r_ref, w_ref, k_ref, nkk_ref, bb_ref, v_ref, s0_ref, o_ref, st_ref, s_scr):
    s_scr[...] = s0_ref[0]

    def o_row(vl, val):
        o_ref[0, pl.ds(vl, 1), :] = val

    _wkv_step(s_scr, r_ref[0], w_ref[0], k_ref[0], nkk_ref[0], bb_ref[0],
              lambda vl: v_ref[0, pl.ds(vl, 1), :], o_row)
    st_ref[0] = s_scr[...]


def _wkv_scan(r, w, k, nkk, bb, v, s0):
    groups = r.shape[0]
    key_spec = pl.BlockSpec((1, HEAD, LANES), lambda g: (g, 0, 0))
    val_spec = pl.BlockSpec((1, V_ROWS, LANES), lambda g: (g, 0, 0))
    st_spec = pl.BlockSpec((1, V_ROWS, HEAD, LANES), lambda g: (g, 0, 0, 0))
    return pl.pallas_call(
        _wkv_kernel,
        grid=(groups,),
        in_specs=[key_spec] * 5 + [val_spec, st_spec],
        out_specs=[val_spec, st_spec],
        out_shape=[jax.ShapeDtypeStruct((groups, V_ROWS, LANES), F32),
                   jax.ShapeDtypeStruct((groups, V_ROWS, HEAD, LANES), F32)],
        scratch_shapes=[pltpu.VMEM((V_ROWS, HEAD, LANES), F32)],
        compiler_params=_params(("parallel",)),
        name="wkv_scan",
    )(r, w, k, nkk, bb, v, s0)


def _to_scan_keys(x):
    g = x.shape[0] // BATCH
    x = x.reshape(g, BATCH, HEAD, HEADS).transpose(0, 2, 1, 3).reshape(g, HEAD, BATCH * HEADS)
    return jnp.concatenate([x, x], axis=-1)


def _to_scan_vals(x):
    g = x.shape[0] // BATCH
    return x.reshape(g, BATCH, 2, V_ROWS, HEADS).transpose(0, 3, 2, 1, 4).reshape(g, V_ROWS, LANES)


def _from_scan_vals(o):
    g = o.shape[0]
    return o.reshape(g, V_ROWS, 2, BATCH, HEADS).transpose(0, 3, 2, 1, 4).reshape(g * BATCH, RWKV_WIDTH)


def _state_to_scan(s):
    g = s.shape[0] // BATCH
    s = s.reshape(g, BATCH, HEADS, 2, V_ROWS, HEAD).transpose(0, 4, 5, 3, 1, 2)
    return s.reshape(g, V_ROWS, HEAD, LANES)


def _state_from_scan(s):
    g = s.shape[0]
    s = s.reshape(g, V_ROWS, HEAD, 2, BATCH, HEADS).transpose(0, 4, 5, 3, 1, 2)
    return s.reshape(g * BATCH, HEADS, HEAD, HEAD)


def _seq_state_from_scan(s):
    s = s.reshape(V_ROWS, HEAD, BATCH, 2, HEADS).transpose(2, 4, 3, 0, 1)
    return s.reshape(BATCH, HEADS, HEAD, HEAD)


def _mix_out_kernel(o_ref, g_ref, bv_ref, ycv_ref, h_ref, gng_ref, gnb_ref, ones_ref, wo1_ref, wo2_ref,
                    lg_ref, lb_ref, *rest, o_channel_major):
    h1_ref, h1b_ref = rest[-2:]
    ones_hd = ones_ref[...]
    o = o_ref[0].T if o_channel_major else o_ref[...]
    d = o - _headsum(o, ones_hd) * (1.0 / HEAD)
    var = _headsum(d * d, ones_hd) * (1.0 / HEAD)
    gn = d * lax.rsqrt(var + GN_EPS) * gng_ref[...] + gnb_ref[...]
    y_rw = (gn + bv_ref[...]) * g_ref[...]
    mix = (jnp.dot(ycv_ref[...].astype(BF16), wo1_ref[...], preferred_element_type=F32)
           + jnp.dot(y_rw.astype(BF16), wo2_ref[...], preferred_element_type=F32))
    h1 = _layer_norm(ALPHA * h_ref[...] + mix, lg_ref[...], lb_ref[...])
    h1_ref[...] = h1
    h1b_ref[...] = h1.astype(BF16)


def _mix_out_call(kernel, grid, o_spec, tok_map, out_map, n_out, tb, args):
    fixed = lambda *_: (0, 0)
    tok512 = pl.BlockSpec((tb, RWKV_WIDTH), tok_map)
    return pl.pallas_call(
        kernel,
        grid=grid,
        in_specs=[o_spec, tok512, tok512, tok512, pl.BlockSpec((tb, D_MODEL), tok_map),
                  pl.BlockSpec((1, RWKV_WIDTH), fixed), pl.BlockSpec((1, RWKV_WIDTH), fixed),
                  pl.BlockSpec((RWKV_WIDTH, RWKV_WIDTH), fixed),
                  pl.BlockSpec((CONV_WIDTH, D_MODEL), fixed), pl.BlockSpec((RWKV_WIDTH, D_MODEL), fixed),
                  pl.BlockSpec((1, D_MODEL), fixed), pl.BlockSpec((1, D_MODEL), fixed)],
        out_specs=[pl.BlockSpec((tb, D_MODEL), out_map), pl.BlockSpec((tb, D_MODEL), out_map)],
        out_shape=[jax.ShapeDtypeStruct((n_out, D_MODEL), F32), jax.ShapeDtypeStruct((n_out, D_MODEL), BF16)],
        compiler_params=_params(("parallel",) * len(grid)),
        name="mix_out",
    )(*args)


def _mix_out_seq(o, g, bv, ycv, h, out_params):
    per_seq = S_PITCH // TB_OUT
    real = SEQ // TB_OUT
    return _mix_out_call(functools.partial(_mix_out_kernel, o_channel_major=True), (BATCH, real),
                         pl.BlockSpec((1, RWKV_WIDTH, TB_OUT), lambda b, j: (b, 0, j)),
                         lambda b, j: (b * per_seq + j, 0), lambda b, j: (b * real + j, 0),
                         T_PEER, TB_OUT, (o, g, bv, ycv, h, *out_params))


def _mix_out_tail(o, g, bv, ycv, h, out_params, h1, h1b):
    tok = lambda i: (0, 0)
    fixed = lambda *_: (0, 0)
    last = lambda i: (T_PEER // TB_PEER - 1, 0)
    tok512 = pl.BlockSpec((TB_PEER, RWKV_WIDTH), tok)
    whole = pl.BlockSpec(memory_space=pl.ANY)
    n_in = 5 + len(out_params)
    return pl.pallas_call(
        functools.partial(_mix_out_kernel, o_channel_major=False),
        grid=(1,),
        in_specs=[tok512, tok512, tok512, tok512, pl.BlockSpec((TB_PEER, D_MODEL), lambda i: (T_SEQS // TB_PEER, 0)),
                  pl.BlockSpec((1, RWKV_WIDTH), fixed), pl.BlockSpec((1, RWKV_WIDTH), fixed),
                  pl.BlockSpec((RWKV_WIDTH, RWKV_WIDTH), fixed),
                  pl.BlockSpec((CONV_WIDTH, D_MODEL), fixed), pl.BlockSpec((RWKV_WIDTH, D_MODEL), fixed),
                  pl.BlockSpec((1, D_MODEL), fixed), pl.BlockSpec((1, D_MODEL), fixed), whole, whole],
        out_specs=[pl.BlockSpec((TB_PEER, D_MODEL), last), pl.BlockSpec((TB_PEER, D_MODEL), last)],
        out_shape=[jax.ShapeDtypeStruct(h1.shape, F32), jax.ShapeDtypeStruct(h1b.shape, BF16)],
        input_output_aliases={n_in: 0, n_in + 1: 1},
        compiler_params=_params(("arbitrary",)),
        name="mix_out_tail",
    )(o, g, bv, ycv, h, *out_params, h1, h1b)


CAND_ROWS = TOPK + 7 * 8 + 8


def _extract_top(s, ids, count, break_ties):
    rank = jnp.full(s.shape, NOT_RANKED, F32)
    big = jnp.float32(NOT_RANKED)
    vals = []
    for p in range(count):
        m = jnp.max(s, axis=0, keepdims=True)
        if break_ties:
            hit = ids == jnp.min(jnp.where(s == m, ids, big), axis=0, keepdims=True)
        else:
            hit = s == m
        rank = jnp.where(hit, jnp.float32(p), rank)
        s = jnp.where(hit, -jnp.inf, s)
        vals.append(m)
    return vals, rank


def _all_distinct(rank, count):
    picked = jnp.sum(jnp.where(rank < float(count), 1.0, 0.0), axis=0, keepdims=True)
    return jnp.where(picked == float(count), 1.0, 0.0)


def _peer_route_kernel(x_ref, wqt_ref, keys_ref, n1_ref, e1_ref, r2_ref, e2_ref, qt_ref):
    qt_ref[...] = lax.dot_general(wqt_ref[...], x_ref[...], (((1,), (1,)), ((), ())),
                                  preferred_element_type=F32)
    lanes = x_ref.shape[0]
    ok = _route_heads(qt_ref, keys_ref, n1_ref, e1_ref, r2_ref, e2_ref, lanes, break_ties=False)

    @pl.when(jnp.min(ok) < 0.5)
    def _():
        _route_heads(qt_ref, keys_ref, n1_ref, e1_ref, r2_ref, e2_ref, lanes, break_ties=True)


def _route_heads(qt_ref, keys_ref, n1_ref, e1_ref, r2_ref, e2_ref, lanes, break_ties):
    key_ids = lax.broadcasted_iota(jnp.int32, (N_KEYS, lanes), 0).astype(F32)
    row16 = lax.broadcasted_iota(jnp.int32, (TOPK, lanes), 0).astype(F32)
    row8 = lax.broadcasted_iota(jnp.int32, (8, lanes), 0).astype(F32)
    cand_ids = jnp.concatenate([row16] + [row8 + float(TOPK * p) for p in range(1, 8)]
                               + [(row8 + 8.0) * float(TOPK)], axis=0)
    ok = jnp.ones((1, lanes), F32)
    for h in range(PEER_HEADS):
        s1 = jnp.dot(keys_ref[2 * h], qt_ref[(2 * h) * HALF:(2 * h + 1) * HALF, :].astype(BF16),
                     preferred_element_type=F32)
        s2 = jnp.dot(keys_ref[2 * h + 1], qt_ref[(2 * h + 1) * HALF:(2 * h + 2) * HALF, :].astype(BF16),
                     preferred_element_type=F32)
        a_vals, rank1 = _extract_top(s1, key_ids, TOPK, break_ties)
        b_vals, rank2 = _extract_top(s2, key_ids, TOPK, break_ties)
        b = jnp.zeros((TOPK, lanes), F32)
        a_hi = jnp.zeros((8, lanes), F32)
        for p in range(TOPK):
            b = jnp.where(row16 == float(p), b_vals[p], b)
        for p in range(8):
            a_hi = jnp.where(row8 == float(p), a_vals[8 + p], a_hi)
        cand = jnp.concatenate([a_vals[0] + b] + [a_vals[p] + b[0:8, :] for p in range(1, 8)]
                               + [a_hi + b_vals[0]], axis=0)
        _, crank = _extract_top(cand, cand_ids, TOPK, break_ties)
        if not break_ties:
            ok = ok * _all_distinct(rank1, TOPK) * _all_distinct(rank2, TOPK) * _all_distinct(crank, TOPK)
        sel = crank < float(TOPK)
        top = a_vals[0] + b_vals[0]
        z = jnp.sum(jnp.where(sel, jnp.exp(cand - top), 0.0), axis=0, keepdims=True)
        self32 = jnp.where(sel, 1.0, 0.0)
        counts = [jnp.sum(self32[0:TOPK, :], axis=0, keepdims=True)]
        counts += [jnp.sum(self32[TOPK + 8 * (p - 1):TOPK + 8 * p, :], axis=0, keepdims=True) for p in range(1, 8)]
        tail = self32[TOPK + 56:TOPK + 64, :]
        counts += [jnp.sum(jnp.where(row8 == float(p), tail, 0.0), axis=0, keepdims=True) for p in range(8)]
        n1 = jnp.zeros((N_KEYS, lanes), F32)
        for p in range(TOPK):
            n1 = jnp.where(rank1 == float(p), counts[p], n1)
        n1_ref[h] = n1
        e1_ref[h] = jnp.exp(s1 - a_vals[0])
        r2_ref[h] = rank2.astype(BF16)
        e2_ref[h] = (jnp.exp(s2 - b_vals[0]) * (0.5 / z)).astype(BF16)
    return ok


def _peer_route(h1b, wqt, keys):
    n = h1b.shape[0]
    maps = pl.BlockSpec((PEER_HEADS, N_KEYS, TB_ROUTE), lambda i: (0, 0, i))
    f32_map = jax.ShapeDtypeStruct((PEER_HEADS, N_KEYS, n), F32)
    bf16_map = jax.ShapeDtypeStruct((PEER_HEADS, N_KEYS, n), BF16)
    return pl.pallas_call(
        _peer_route_kernel,
        grid=(n // TB_ROUTE,),
        in_specs=[pl.BlockSpec((TB_ROUTE, D_MODEL), lambda i: (i, 0)),
                  pl.BlockSpec((2 * PEER_HEADS * HALF, D_MODEL), lambda i: (0, 0)),
                  pl.BlockSpec((2 * PEER_HEADS, N_KEYS, HALF), lambda i: (0, 0, 0))],
        out_specs=[maps] * 4,
        out_shape=[f32_map, f32_map, bf16_map, bf16_map],
        scratch_shapes=[pltpu.VMEM((2 * PEER_HEADS * HALF, TB_ROUTE), F32)],
        compiler_params=_params(("parallel",)),
        name="peer_route",
    )(h1b, wqt, keys)


SLAB = 16


def _peer_dense_kernel(xb_ref, x_ref, u_ref, vt_ref, n1_ref, e1_ref, r2_ref, e2_ref, lg_ref, lb_ref,
                       y_ref, ytail_ref, acc_ref, ht_ref, act_ref):
    c = pl.program_id(1)

    @pl.when(c == 0)
    def _():
        acc_ref[...] = jnp.zeros_like(acc_ref)

    lanes = ht_ref.shape[1]
    zero = jnp.zeros((SLAB, lanes), BF16)
    xb = xb_ref[...]
    for sub in range(I_PER_CHUNK // I_PER_SUB):
        sub_rows = slice(sub * E_SUB, (sub + 1) * E_SUB)
        ht_ref[sub_rows, :] = lax.dot_general(u_ref[sub_rows, :], xb, (((1,), (1,)), ((), ())),
                                              preferred_element_type=F32)
    for sub in range(I_PER_CHUNK // I_PER_SUB):
        sub_rows = slice(sub * E_SUB, (sub + 1) * E_SUB)
        for ii in range(sub * I_PER_SUB, (sub + 1) * I_PER_SUB):
            n1 = [jnp.broadcast_to(n1_ref[h, 0, ii:ii + 1, :], (SLAB, lanes)).astype(BF16)
                  for h in range(PEER_HEADS)]
            e1 = [jnp.broadcast_to(e1_ref[h, 0, ii:ii + 1, :], (SLAB, lanes)).astype(BF16)
                  for h in range(PEER_HEADS)]
            for js in range(N_KEYS // SLAB):
                rows = slice(js * SLAB, (js + 1) * SLAB)
                gate = jnp.where(r2_ref[0, rows, :] < n1[0], e2_ref[0, rows, :], zero) * e1[0]
                for h in range(1, PEER_HEADS):
                    gate = gate + jnp.where(r2_ref[h, rows, :] < n1[h], e2_ref[h, rows, :], zero) * e1[h]
                out_rows = slice(ii * N_KEYS + js * SLAB, ii * N_KEYS + (js + 1) * SLAB)
                ht = ht_ref[out_rows, :]
                gelu2 = ht * (1.0 + lax.erf(ht * SQRT_HALF))
                act_ref[out_rows, :] = gelu2.astype(BF16) * gate
        acc_ref[...] += jnp.dot(vt_ref[:, sub_rows], act_ref[sub_rows, :], preferred_element_type=F32)

    is_tail = pl.program_id(0) == pl.num_programs(0) - 1

    @pl.when(c == pl.num_programs(1) - 1)
    def _():
        y = _layer_norm(ALPHA * x_ref[...] + acc_ref[...].T, lg_ref[...], lb_ref[...])

        @pl.when(jnp.logical_not(is_tail))
        def _():
            y_ref[...] = y

        @pl.when(is_tail)
        def _():
            ytail_ref[...] = y


def _peer_dense(h1b, h1, u, vt, n1, e1, r2, e2, lg, lb):
    n = h1b.shape[0]
    n_blocks = n // TB_PEER
    n_chunks = N_EXPERTS // E_CHUNK
    tok = pl.BlockSpec((TB_PEER, D_MODEL), lambda i, c: (i, 0))
    by_first = pl.BlockSpec((PEER_HEADS, 1, I_PER_CHUNK, TB_PEER), lambda i, c: (0, c, 0, i))
    by_second = pl.BlockSpec((PEER_HEADS, N_KEYS, TB_PEER), lambda i, c: (0, 0, i))
    vec = pl.BlockSpec((1, D_MODEL), lambda i, c: (0, 0))
    return pl.pallas_call(
        _peer_dense_kernel,
        grid=(n // TB_PEER, n_chunks),
        in_specs=[tok, tok,
                  pl.BlockSpec((E_CHUNK, D_MODEL), lambda i, c: (c, 0)),
                  pl.BlockSpec((D_MODEL, E_CHUNK), lambda i, c: (0, c)),
                  by_first, by_first, by_second, by_second, vec, vec],
        out_specs=[pl.BlockSpec((TB_PEER, D_MODEL), lambda i, c: (jnp.minimum(i, n_blocks - 2), 0)),
                   pl.BlockSpec((TB_PEER, D_MODEL), lambda i, c: (0, 0))],
        out_shape=[jax.ShapeDtypeStruct((n - TB_PEER, D_MODEL), F32),
                   jax.ShapeDtypeStruct((TB_PEER, D_MODEL), F32)],
        scratch_shapes=[pltpu.VMEM((D_MODEL, TB_PEER), F32), pltpu.VMEM((E_CHUNK, TB_PEER), F32),
                        pltpu.VMEM((E_CHUNK, TB_PEER), BF16)],
        compiler_params=_params(("arbitrary", "arbitrary")),
        name="peer_dense",
    )(h1b, h1, u, vt, n1, e1, r2, e2, lg, lb)


def _pad_cols(x, width):
    return jnp.pad(x, ((0, 0), (0, width - x.shape[1])))


def _pad_rows(x, height):
    return jnp.pad(x, ((0, height - x.shape[0]), (0, 0)))


def kernel(x_prompt, x_sample, state_conv, state_shift, state_wkv, meta_tokens, ln_in_g, ln_in_b, w_in, conv_w, mu, w0, w_up, a0, a_up, g_up, k_k, k_a, r_k, gn_g, gn_b, w_o, ln1_g, ln1_b, w_q, sub_keys, peer_u, peer_v, ln2_g, ln2_b):
    row = lambda x: x.reshape(1, -1)
    chan = np.arange(RWKV_WIDTH)
    perm = (chan % HEADS) * HEAD + chan // HEADS
    pc = lambda x: x[..., perm]
    w_in0 = w_in[0]
    wcv = w_in0[:, :CV_COLS].astype(BF16)
    rw0 = CV_COLS
    lo0 = rw0 + 3 * RWKV_WIDTH

    def rw_layout(x):
        groups = [pc(x[:, rw0 + i * RWKV_WIDTH:rw0 + (i + 1) * RWKV_WIDTH]) for i in range(3)]
        return jnp.concatenate(groups + [_pad_cols(x[:, lo0:lo0 + W_LORA], LORA_SLOT),
                                         _pad_cols(x[:, lo0 + W_LORA:lo0 + W_LORA + A_LORA], LORA_SLOT),
                                         _pad_cols(x[:, lo0 + W_LORA + A_LORA:], LORA_SLOT)], axis=1)

    wrw = rw_layout(w_in0).astype(BF16)
    mu_p = rw_layout(jnp.pad(mu, ((0, 0), (CV_COLS, 0))))
    ones_hd = jnp.asarray((chan[:, None] % HEADS) == (chan[None, :] % HEADS), BF16)
    lora = lambda w: _pad_rows(pc(w[0]), LORA_SLOT).astype(BF16)
    mix_params = (conv_w[0], mu_p, pc(w0), lora(w_up), pc(a0), lora(a_up), lora(g_up),
                  pc(k_k), pc(k_a), pc(r_k.reshape(1, RWKV_WIDTH)), ones_hd)
    out_params = (pc(gn_g), pc(gn_b), ones_hd, w_o[0, :CONV_WIDTH].astype(BF16),
                  w_o[0, CONV_WIDTH:][perm].astype(BF16), ln1_g, ln1_b)

    dt = x_prompt.dtype
    seq_pad = jnp.zeros((BATCH, S_PITCH - SEQ - N_META, D_MODEL), dt)
    meta = jnp.broadcast_to(meta_tokens.astype(dt)[None], (BATCH, N_META, D_MODEL))
    x_all = jnp.concatenate([jnp.concatenate([x_prompt, seq_pad, meta], axis=1).reshape(T_SEQS, D_MODEL),
                             x_sample.reshape(DEC_BATCH, D_MODEL),
                             jnp.zeros((T_PHYS - T_SEQS - DEC_BATCH, D_MODEL), dt)], axis=0)
    h, pcv, prw = _ln_proj(x_all, row(ln_in_g), row(ln_in_b), wcv, wrw)
    prev_rw_s = _shift_proj(state_shift[0], wrw)

    outs_p = _mixer_seq(pcv, prw, jnp.zeros((BATCH, 8, CONV_WIDTH), F32), jnp.zeros((BATCH, 8, RW_PAD), F32),
                        mix_params)
    scan_p = outs_p[:N_SCAN_IN]
    ycv_p, g_p, bv_p = outs_p[N_SCAN_IN:N_SCAN_IN + N_TOK_OUT]
    conv_p = outs_p[N_SCAN_IN + N_TOK_OUT][:, 6:8]
    o_p, st_p = _wkv_seq(scan_p, jnp.zeros((V_ROWS, HEAD, LANES), F32))
    wkv_p = _seq_state_from_scan(st_p)
    h1_p, h1b_p = _mix_out_seq(o_p, g_p, bv_p, ycv_p, h, out_params)

    s_rows = slice(T_SEQS, T_SEQS + DEC_BATCH)
    outs_s = _mixer_rows(pcv[s_rows], prw[s_rows], prev_rw_s, state_conv[0, :, 1], state_conv[0, :, 0], mix_params)
    r_s, w_s, k_s, nkk_s, bb_s, v_s, ycv_s, g_s, bv_s, z_s = outs_s
    o_s, st_s = _wkv_scan(_to_scan_keys(r_s), _to_scan_keys(w_s), _to_scan_keys(k_s), _to_scan_keys(nkk_s),
                          _to_scan_keys(bb_s), _to_scan_vals(v_s), _state_to_scan(state_wkv[0]))
    wkv_s = _state_from_scan(st_s)
    tail = lambda x: _pad_rows(x, TB_PEER)
    h1, h1b = _mix_out_tail(tail(_from_scan_vals(o_s)), tail(g_s), tail(bv_s), tail(ycv_s), h, out_params,
                            h1_p, h1b_p)

    wqt = w_q[0].T.astype(BF16)
    keys = sub_keys[0].reshape(2 * PEER_HEADS, N_KEYS, HALF).astype(BF16)
    n1, e1, r2, e2 = _peer_route(h1b, wqt, keys)
    by_first = lambda x: x.reshape(PEER_HEADS, N_KEYS // I_PER_CHUNK, I_PER_CHUNK, T_PEER)
    y, y_tail = _peer_dense(h1b, h1, peer_u[0].astype(BF16), peer_v[0].T.astype(BF16), by_first(n1), by_first(e1),
                            r2, e2, row(ln2_g), row(ln2_b))

    y_prompt = y.reshape(BATCH, SEQ, D_MODEL)
    y_sample = y_tail[:DEC_BATCH].reshape(DEC_BATCH, 1, D_MODEL)
    shift_p = h[:T_SEQS].reshape(BATCH, S_PITCH, D_MODEL)[:, SEQ - 1]
    conv_s = jnp.stack([state_conv[0, :, 1], z_s], axis=1)
    return (y_prompt, y_sample, conv_p[None], shift_p[None], wkv_p[None],
            conv_s[None], h[s_rows][None], wkv_s[None])
```

```python
import functools

import numpy as np
import jax
import jax.numpy as jnp
from jax import lax
from jax.experimental import pallas as pl
from jax.experimental.pallas import tpu as pltpu

F32 = jnp.float32
BF16 = jnp.bfloat16

D_MODEL = 1024
N_META = 16
BATCH = 8
SEQ = 2048
DEC_BATCH = 128
T_BLK = 128
S_PITCH = 2304
N_TIME_BLK = SEQ // T_BLK + 1
META_BLK = S_PITCH // T_BLK - 1
T_SEQS = BATCH * S_PITCH
T_PHYS = T_SEQS + 512
T_REAL = BATCH * SEQ + DEC_BATCH
T_PEER = 16896
CONV_WIDTH = 512
RWKV_WIDTH = 512
HEAD = 64
HEADS = 8
W_LORA = 32
A_LORA = 32
G_LORA = 96
CV_COLS = 3 * CONV_WIDTH
LORA_SLOT = 128
RW_PAD = 3 * RWKV_WIDTH + 3 * LORA_SLOT
PEER_HEADS = 8
N_KEYS = 128
N_EXPERTS = N_KEYS * N_KEYS
TOPK = 16
HALF = 128
LN_EPS = 1e-5
GN_EPS = HEAD * 1e-5
ALPHA = 2.0 ** 0.25
SQRT_HALF = float(np.sqrt(0.5))
NOT_RANKED = 256.0
VMEM_LIMIT_BYTES = 56 * 1024 * 1024

TB_PROJ = 256
TB_OUT = 256
TB_ROUTE = 256
TB_PEER = 512
I_PER_CHUNK = 16
E_CHUNK = I_PER_CHUNK * N_KEYS
I_PER_SUB = 4
E_SUB = I_PER_SUB * N_KEYS


def _params(sem):
    return pltpu.CompilerParams(dimension_semantics=sem, vmem_limit_bytes=VMEM_LIMIT_BYTES)


def _layer_norm(x, g, b):
    m = jnp.mean(x, axis=-1, keepdims=True)
    xc = x - m
    var = jnp.mean(xc * xc, axis=-1, keepdims=True)
    return xc * lax.rsqrt(var + LN_EPS) * g + b


def _headsum(x, ones_hd):
    hi = x.astype(BF16)
    lo = (x - hi.astype(F32)).astype(BF16)
    return (jnp.dot(hi, ones_hd, preferred_element_type=F32)
            + jnp.dot(lo, ones_hd, preferred_element_type=F32))


def _time_block(l):
    return jnp.where(l == 0, META_BLK, l - 1)


S_CHAN = N_TIME_BLK * T_BLK


def _chan_block(l):
    return jnp.where(l == 0, N_TIME_BLK - 1, l - 1)


PROJ_PER_SEQ = S_PITCH // TB_PROJ
PROJ_REAL = SEQ // TB_PROJ
PROJ_SEQ_BLOCKS = BATCH * PROJ_PER_SEQ


def _ln_proj_kernel(xp_ref, xe_ref, g_ref, b_ref, wcv_ref, wrw_ref, h_ref, pcv_ref, prw_ref):
    k = pl.program_id(0)
    is_real = jnp.logical_and(k < PROJ_SEQ_BLOCKS, k % PROJ_PER_SEQ < PROJ_REAL)
    x = jnp.where(is_real, xp_ref[0], xe_ref[0])
    h = _layer_norm(x, g_ref[...], b_ref[...])
    h_ref[...] = h
    hb = h.astype(BF16)
    pcv_ref[...] = jnp.dot(hb, wcv_ref[...], preferred_element_type=F32)
    prw_ref[...] = jnp.dot(hb, wrw_ref[...], preferred_element_type=F32)


def _ln_proj(x_prompt, x_extra, g, b, wcv, wrw):
    n = T_PHYS
    row = lambda i: (i, 0)
    fixed = lambda i: (0, 0)
    prompt = lambda k: (jnp.minimum(k // PROJ_PER_SEQ, BATCH - 1), jnp.minimum(k % PROJ_PER_SEQ, PROJ_REAL - 1), 0)
    extra = lambda k: (jnp.maximum(k - PROJ_SEQ_BLOCKS + 1, 0), 0, 0)
    return pl.pallas_call(
        _ln_proj_kernel,
        grid=(n // TB_PROJ,),
        in_specs=[pl.BlockSpec((1, TB_PROJ, D_MODEL), prompt),
                  pl.BlockSpec((1, TB_PROJ, D_MODEL), extra),
                  pl.BlockSpec((1, D_MODEL), fixed),
                  pl.BlockSpec((1, D_MODEL), fixed),
                  pl.BlockSpec((D_MODEL, CV_COLS), fixed),
                  pl.BlockSpec((D_MODEL, RW_PAD), fixed)],
        out_specs=[pl.BlockSpec((TB_PROJ, D_MODEL), row),
                   pl.BlockSpec((TB_PROJ, CV_COLS), row),
                   pl.BlockSpec((TB_PROJ, RW_PAD), row)],
        out_shape=[jax.ShapeDtypeStruct((n, D_MODEL), F32),
                   jax.ShapeDtypeStruct((n, CV_COLS), F32),
                   jax.ShapeDtypeStruct((n, RW_PAD), F32)],
        compiler_params=_params(("parallel",)),
        name="ln_proj",
    )(x_prompt, x_extra, g, b, wcv, wrw)


def _shift_proj_kernel(x_ref, wrw_ref, prw_ref):
    prw_ref[...] = jnp.dot(x_ref[...].astype(BF16), wrw_ref[...], preferred_element_type=F32)


def _shift_proj(x, wrw):
    n = x.shape[0]
    return pl.pallas_call(
        _shift_proj_kernel,
        out_shape=jax.ShapeDtypeStruct((n, RW_PAD), F32),
        compiler_params=_params(None),
        name="shift_proj",
    )(x, wrw)


N_SCAN_IN = 6
N_TOK_OUT = 3


def _mixer_core(pcv, prw, prw_prev, z1, z2, cw, mu, w0, wup, a0, aup, gup, kkw, kaw, rkw, ones_hd):
    bg = pcv[:, 0:CONV_WIDTH]
    z = pcv[:, CONV_WIDTH:2 * CONV_WIDTH] * pcv[:, 2 * CONV_WIDTH:3 * CONV_WIDTH]
    conv = cw[0:1, :] * z2 + cw[1:2, :] * z1 + cw[2:3, :] * z
    y_cv = bg * conv

    m = prw + (prw_prev - prw) * mu
    r = m[:, 0:RWKV_WIDTH]
    k = m[:, RWKV_WIDTH:2 * RWKV_WIDTH]
    v = m[:, 2 * RWKV_WIDTH:3 * RWKV_WIDTH]
    base = 3 * RWKV_WIDTH
    wd = m[:, base:base + LORA_SLOT]
    ad = m[:, base + LORA_SLOT:base + 2 * LORA_SLOT]
    gd = m[:, base + 2 * LORA_SLOT:base + 3 * LORA_SLOT]

    xw = -(w0 + jnp.dot(jnp.tanh(wd).astype(BF16), wup, preferred_element_type=F32))
    softplus = jnp.maximum(xw, 0.0) + jnp.log1p(jnp.exp(-jnp.abs(xw)))
    w_log = -softplus - 0.5
    decay = jnp.exp(-jnp.exp(w_log))
    a = jax.nn.sigmoid(a0 + jnp.dot(ad.astype(BF16), aup, preferred_element_type=F32))
    g = jnp.dot(jax.nn.sigmoid(gd).astype(BF16), gup, preferred_element_type=F32)

    kk = k * kkw
    norm = jnp.sqrt(_headsum(kk * kk, ones_hd))
    kk = kk / jnp.maximum(norm, 1e-12)
    kx = k * (1.0 + (a - 1.0) * kaw)
    bonus_v = _headsum(r * kx * rkw, ones_hd) * v
    return z, (r, decay, kx, -kk, kk * a, v), (y_cv, g, bonus_v)


def _mixer_seq_kernel(pcv_ref, prw_ref, conv0_ref, prw0_ref, cw_ref, mu_ref, w0_ref, wup_ref, a0_ref, aup_ref,
                      gup_ref, kkw_ref, kaw_ref, rkw_ref, ones_ref, *rest):
    scan_refs = rest[:N_SCAN_IN]
    tok_refs = rest[N_SCAN_IN:N_SCAN_IN + N_TOK_OUT]
    convt_ref = rest[N_SCAN_IN + N_TOK_OUT]
    zc_ref, pc_ref = rest[N_SCAN_IN + N_TOK_OUT + 1:]
    l = pl.program_id(1)

    @pl.when(l == 0)
    def _():
        zc_ref[...] = conv0_ref[0]
        pc_ref[...] = prw0_ref[0]

    pcv = pcv_ref[...]
    prw = prw_ref[...]
    n = pcv.shape[0]
    first = jnp.where(l == 0, n - N_META, 0)
    z = pcv[:, CONV_WIDTH:2 * CONV_WIDTH] * pcv[:, 2 * CONV_WIDTH:3 * CONV_WIDTH]
    row_z = lax.broadcasted_iota(jnp.int32, z.shape, 0)
    row_p = lax.broadcasted_iota(jnp.int32, prw.shape, 0)
    c2 = zc_ref[6:7, :]
    c1 = zc_ref[7:8, :]
    z1 = jnp.where(row_z == first, c1, pltpu.roll(z, 1, axis=0))
    z2 = jnp.where(row_z == first, c2, jnp.where(row_z == first + 1, c1, pltpu.roll(z, 2, axis=0)))
    prw_prev = jnp.where(row_p == first, pc_ref[7:8, :], pltpu.roll(prw, 1, axis=0))

    _, scan_in, tok_out = _mixer_core(pcv, prw, prw_prev, z1, z2, cw_ref[...], mu_ref[...], w0_ref[...],
                                      wup_ref[...], a0_ref[...], aup_ref[...], gup_ref[...], kkw_ref[...],
                                      kaw_ref[...], rkw_ref[...], ones_ref[...])
    for ref, val in zip(scan_refs, scan_in):
        ref[0] = val.T
    for ref, val in zip(tok_refs, tok_out):
        ref[...] = val
    tail = z[n - 8:n, :]
    zc_ref[...] = tail
    pc_ref[...] = prw[n - 8:n, :]
    convt_ref[0] = tail


def _mixer_row_kernel(pcv_ref, prw_ref, prwprev_ref, z1_ref, z2_ref, cw_ref, mu_ref, w0_ref, wup_ref, a0_ref,
                      aup_ref, gup_ref, kkw_ref, kaw_ref, rkw_ref, ones_ref, *rest):
    out_refs = rest[:N_SCAN_IN + N_TOK_OUT]
    z_ref = rest[N_SCAN_IN + N_TOK_OUT]
    z, scan_in, tok_out = _mixer_core(pcv_ref[...], prw_ref[...], prwprev_ref[...], z1_ref[...], z2_ref[...],
                                      cw_ref[...], mu_ref[...], w0_ref[...], wup_ref[...], a0_ref[...],
                                      aup_ref[...], gup_ref[...], kkw_ref[...], kaw_ref[...], rkw_ref[...],
                                      ones_ref[...])
    for ref, val in zip(out_refs, scan_in + tok_out):
        ref[...] = val
    z_ref[...] = z


def _mixer_param_specs():
    fixed = (lambda *_: (0, 0))
    shapes = [(3, CONV_WIDTH), (1, RW_PAD), (1, RWKV_WIDTH), (LORA_SLOT, RWKV_WIDTH), (1, RWKV_WIDTH),
              (LORA_SLOT, RWKV_WIDTH), (LORA_SLOT, RWKV_WIDTH), (1, RWKV_WIDTH), (1, RWKV_WIDTH), (1, RWKV_WIDTH),
              (RWKV_WIDTH, RWKV_WIDTH)]
    return [pl.BlockSpec(s, fixed) for s in shapes]


def _mixer_seq(pcv, prw, conv0, prw0, mix_params):
    blocks_per_seq = S_PITCH // T_BLK
    row = lambda b, l: (b * blocks_per_seq + _time_block(l), 0)
    chan = lambda b, l: (b, 0, _chan_block(l))
    seq = lambda b, l: (b, 0, 0)
    return pl.pallas_call(
        _mixer_seq_kernel,
        grid=(BATCH, N_TIME_BLK),
        in_specs=[pl.BlockSpec((T_BLK, CV_COLS), row),
                  pl.BlockSpec((T_BLK, RW_PAD), row),
                  pl.BlockSpec((1, 8, CONV_WIDTH), seq),
                  pl.BlockSpec((1, 8, RW_PAD), seq)] + _mixer_param_specs(),
        out_specs=[pl.BlockSpec((1, RWKV_WIDTH, T_BLK), chan)] * N_SCAN_IN
                  + [pl.BlockSpec((T_BLK, RWKV_WIDTH), row)] * N_TOK_OUT
                  + [pl.BlockSpec((1, 8, CONV_WIDTH), seq)],
        out_shape=[jax.ShapeDtypeStruct((BATCH, RWKV_WIDTH, S_CHAN), F32)] * N_SCAN_IN
                  + [jax.ShapeDtypeStruct((T_PHYS, RWKV_WIDTH), F32)] * N_TOK_OUT
                  + [jax.ShapeDtypeStruct((BATCH, 8, CONV_WIDTH), F32)],
        scratch_shapes=[pltpu.VMEM((8, CONV_WIDTH), F32), pltpu.VMEM((8, RW_PAD), F32)],
        compiler_params=_params(("arbitrary", "arbitrary")),
        name="mixer_seq",
    )(pcv, prw, conv0, prw0, *mix_params)


def _mixer_rows(pcv, prw, prw_prev, z1, z2, mix_params):
    n = pcv.shape[0]
    return pl.pallas_call(
        _mixer_row_kernel,
        out_shape=[jax.ShapeDtypeStruct((n, RWKV_WIDTH), F32)] * (N_SCAN_IN + N_TOK_OUT + 1),
        compiler_params=_params(None),
        name="mixer_rows",
    )(pcv, prw, prw_prev, z1, z2, *mix_params)


V_ROWS = HEAD // 2
LANES = 128
PITCH_K = T_BLK + 8
PITCH_O = V_ROWS + 8
RETILE_UNROLL = 4


def _wkv_step(s_scr, r, w, kx, nkk, bb, v_row, o_row):
    for vl in range(V_ROWS):
        sv = s_scr[vl]
        sa = jnp.sum(sv * nkk, axis=0, keepdims=True)
        sn = sv * w + sa * bb + v_row(vl) * kx
        s_scr[vl] = sn
        o_row(vl, jnp.sum(sn * r, axis=0, keepdims=True))


def _wkv_seq_kernel(r_ref, w_ref, k_ref, nkk_ref, bb_ref, v_ref, s0_ref, o_ref, st_ref,
                    s_scr, rs, ws, ks, ns, bs, vs, os_):
    l = pl.program_id(0)

    @pl.when(l == 0)
    def _():
        s_scr[...] = s0_ref[...]
        os_[...] = jnp.zeros_like(os_)

    def retile(src_ref, dst_ref, idx, second_idx):
        lo = pl.multiple_of(idx * 8, 8)
        hi = pl.multiple_of(second_idx * 8, 8)
        tile = jnp.concatenate([x for b in range(BATCH)
                                for x in (src_ref[b, pl.ds(lo, 8), :], src_ref[b, pl.ds(hi, 8), :])], axis=0)
        dst_ref[pl.ds(pl.multiple_of(idx * PITCH_K, 8), T_BLK), :] = tile.T

    def retile_keys(n, carry):
        for src, dst in ((r_ref, rs), (w_ref, ws), (k_ref, ks), (nkk_ref, ns), (bb_ref, bs)):
            retile(src, dst, n, n)
        return carry

    def retile_vals(vl, carry):
        retile(v_ref, vs, vl, vl + V_ROWS)
        return carry

    lax.fori_loop(0, HEAD, retile_keys, 0, unroll=RETILE_UNROLL)
    lax.fori_loop(0, V_ROWS, retile_vals, 0, unroll=RETILE_UNROLL)

    def step(t, carry):
        def keys(ref):
            return jnp.concatenate([ref[pl.ds(g * 8 * PITCH_K + t, 8, stride=PITCH_K), :]
                                    for g in range(HEAD // 8)], axis=0)

        def o_row(vl, val):
            os_[pl.ds(t * PITCH_O + vl, 1), :] = val

        _wkv_step(s_scr, keys(rs), keys(ws), keys(ks), keys(ns), keys(bs),
                  lambda vl: vs[pl.ds(vl * PITCH_K + t, 1), :], o_row)
        return carry

    lax.fori_loop(jnp.where(l == 0, T_BLK - N_META, 0), T_BLK, step, 0)

    def write_out(vl, carry):
        tile = os_[pl.ds(vl, T_BLK, stride=PITCH_O), :].T
        for b in range(BATCH):
            for hf in range(2):
                src = (2 * b + hf) * 8
                o_ref[b, pl.ds(pl.multiple_of((hf * V_ROWS + vl) * 8, 8), 8), :] = tile[src:src + 8, :]
        return carry

    lax.fori_loop(0, V_ROWS, write_out, 0, unroll=RETILE_UNROLL)

    @pl.when(l == pl.num_programs(0) - 1)
    def _():
        st_ref[...] = s_scr[...]


def _wkv_seq(scan_in, s0):
    blk = pl.BlockSpec((BATCH, RWKV_WIDTH, T_BLK), lambda l: (0, 0, _chan_block(l)))
    blk_in = pl.BlockSpec((BATCH, RWKV_WIDTH, T_BLK), lambda l: (0, 0, _chan_block(l)),
                          pipeline_mode=pl.Buffered(1))
    st = pl.BlockSpec((V_ROWS, HEAD, LANES), lambda l: (0, 0, 0))
    retiled = pltpu.VMEM((HEAD * PITCH_K, LANES), F32)
    return pl.pallas_call(
        _wkv_seq_kernel,
        grid=(N_TIME_BLK,),
        in_specs=[blk_in] * N_SCAN_IN + [st],
        out_specs=[blk, st],
        out_shape=[jax.ShapeDtypeStruct((BATCH, RWKV_WIDTH, S_CHAN), F32),
                   jax.ShapeDtypeStruct((V_ROWS, HEAD, LANES), F32)],
        scratch_shapes=[pltpu.VMEM((V_ROWS, HEAD, LANES), F32)] + [retiled] * 5
                       + [pltpu.VMEM((V_ROWS * PITCH_K, LANES), F32), pltpu.VMEM((T_BLK * PITCH_O, LANES), F32)],
        compiler_params=_params(("arbitrary",)),
        name="wkv_seq",
    )(*scan_in, s0)


def _wkv_kernel(r_ref, w_ref, k_ref, nkk_ref, bb_ref, v_ref, s0_ref, o_ref, st_ref, s_scr):
    s_scr[...] = s0_ref[0]

    def o_row(vl, val):
        o_ref[0, pl.ds(vl, 1), :] = val

    _wkv_step(s_scr, r_ref[0], w_ref[0], k_ref[0], nkk_ref[0], bb_ref[0],
              lambda vl: v_ref[0, pl.ds(vl, 1), :], o_row)
    st_ref[0] = s_scr[...]


def _wkv_scan(r, w, k, nkk, bb, v, s0):
    groups = r.shape[0]
    key_spec = pl.BlockSpec((1, HEAD, LANES), lambda g: (g, 0, 0))
    val_spec = pl.BlockSpec((1, V_ROWS, LANES), lambda g: (g, 0, 0))
    st_spec = pl.BlockSpec((1, V_ROWS, HEAD, LANES), lambda g: (g, 0, 0, 0))
    return pl.pallas_call(
        _wkv_kernel,
        grid=(groups,),
        in_specs=[key_spec] * 5 + [val_spec, st_spec],
        out_specs=[val_spec, st_spec],
        out_shape=[jax.ShapeDtypeStruct((groups, V_ROWS, LANES), F32),
                   jax.ShapeDtypeStruct((groups, V_ROWS, HEAD, LANES), F32)],
        scratch_shapes=[pltpu.VMEM((V_ROWS, HEAD, LANES), F32)],
        compiler_params=_params(("parallel",)),
        name="wkv_scan",
    )(r, w, k, nkk, bb, v, s0)


def _to_scan_keys(x):
    g = x.shape[0] // BATCH
    x = x.reshape(g, BATCH, HEAD, HEADS).transpose(0, 2, 1, 3).reshape(g, HEAD, BATCH * HEADS)
    return jnp.concatenate([x, x], axis=-1)


def _to_scan_vals(x):
    g = x.shape[0] // BATCH
    return x.reshape(g, BATCH, 2, V_ROWS, HEADS).transpose(0, 3, 2, 1, 4).reshape(g, V_ROWS, LANES)


def _from_scan_vals(o):
    g = o.shape[0]
    return o.reshape(g, V_ROWS, 2, BATCH, HEADS).transpose(0, 3, 2, 1, 4).reshape(g * BATCH, RWKV_WIDTH)


def _state_to_scan(s):
    g = s.shape[0] // BATCH
    s = s.reshape(g, BATCH, HEADS, 2, V_ROWS, HEAD).transpose(0, 4, 5, 3, 1, 2)
    return s.reshape(g, V_ROWS, HEAD, LANES)


def _state_from_scan(s):
    g = s.shape[0]
    s = s.reshape(g, V_ROWS, HEAD, 2, BATCH, HEADS).transpose(0, 4, 5, 3, 1, 2)
    return s.reshape(g * BATCH, HEADS, HEAD, HEAD)


def _seq_state_from_scan(s):
    s = s.reshape(V_ROWS, HEAD, BATCH, 2, HEADS).transpose(2, 4, 3, 0, 1)
    return s.reshape(BATCH, HEADS, HEAD, HEAD)


def _mix_out_kernel(o_ref, g_ref, bv_ref, ycv_ref, h_ref, gng_ref, gnb_ref, ones_ref, wo1_ref, wo2_ref,
                    lg_ref, lb_ref, *rest, o_channel_major):
    h1_ref, h1b_ref = rest[-2:]
    ones_hd = ones_ref[...]
    o = o_ref[0].T if o_channel_major else o_ref[...]
    d = o - _headsum(o, ones_hd) * (1.0 / HEAD)
    var = _headsum(d * d, ones_hd) * (1.0 / HEAD)
    gn = d * lax.rsqrt(var + GN_EPS) * gng_ref[...] + gnb_ref[...]
    y_rw = (gn + bv_ref[...]) * g_ref[...]
    mix = (jnp.dot(ycv_ref[...].astype(BF16), wo1_ref[...], preferred_element_type=F32)
           + jnp.dot(y_rw.astype(BF16), wo2_ref[...], preferred_element_type=F32))
    h1 = _layer_norm(ALPHA * h_ref[...] + mix, lg_ref[...], lb_ref[...])
    h1_ref[...] = h1
    h1b_ref[...] = h1.astype(BF16)


def _mix_out_call(kernel, grid, o_spec, tok_map, out_map, n_out, tb, args):
    fixed = lambda *_: (0, 0)
    tok512 = pl.BlockSpec((tb, RWKV_WIDTH), tok_map)
    return pl.pallas_call(
        kernel,
        grid=grid,
        in_specs=[o_spec, tok512, tok512, tok512, pl.BlockSpec((tb, D_MODEL), tok_map),
                  pl.BlockSpec((1, RWKV_WIDTH), fixed), pl.BlockSpec((1, RWKV_WIDTH), fixed),
                  pl.BlockSpec((RWKV_WIDTH, RWKV_WIDTH), fixed),
                  pl.BlockSpec((CONV_WIDTH, D_MODEL), fixed), pl.BlockSpec((RWKV_WIDTH, D_MODEL), fixed),
                  pl.BlockSpec((1, D_MODEL), fixed), pl.BlockSpec((1, D_MODEL), fixed)],
        out_specs=[pl.BlockSpec((tb, D_MODEL), out_map), pl.BlockSpec((tb, D_MODEL), out_map)],
        out_shape=[jax.ShapeDtypeStruct((n_out, D_MODEL), F32), jax.ShapeDtypeStruct((n_out, D_MODEL), BF16)],
        compiler_params=_params(("parallel",) * len(grid)),
        name="mix_out",
    )(*args)


def _mix_out_seq(o, g, bv, ycv, h, out_params):
    per_seq = S_PITCH // TB_OUT
    real = SEQ // TB_OUT
    return _mix_out_call(functools.partial(_mix_out_kernel, o_channel_major=True), (BATCH, real),
                         pl.BlockSpec((1, RWKV_WIDTH, TB_OUT), lambda b, j: (b, 0, j)),
                         lambda b, j: (b * per_seq + j, 0), lambda b, j: (b * real + j, 0),
                         T_PEER, TB_OUT, (o, g, bv, ycv, h, *out_params))


def _mix_out_tail(o, g, bv, ycv, h, out_params, h1, h1b):
    tok = lambda i: (0, 0)
    fixed = lambda *_: (0, 0)
    last = lambda i: (T_PEER // TB_PEER - 1, 0)
    tok512 = pl.BlockSpec((TB_PEER, RWKV_WIDTH), tok)
    whole = pl.BlockSpec(memory_space=pl.ANY)
    n_in = 5 + len(out_params)
    return pl.pallas_call(
        functools.partial(_mix_out_kernel, o_channel_major=False),
        grid=(1,),
        in_specs=[tok512, tok512, tok512, tok512, pl.BlockSpec((TB_PEER, D_MODEL), lambda i: (T_SEQS // TB_PEER, 0)),
                  pl.BlockSpec((1, RWKV_WIDTH), fixed), pl.BlockSpec((1, RWKV_WIDTH), fixed),
                  pl.BlockSpec((RWKV_WIDTH, RWKV_WIDTH), fixed),
                  pl.BlockSpec((CONV_WIDTH, D_MODEL), fixed), pl.BlockSpec((RWKV_WIDTH, D_MODEL), fixed),
                  pl.BlockSpec((1, D_MODEL), fixed), pl.BlockSpec((1, D_MODEL), fixed), whole, whole],
        out_specs=[pl.BlockSpec((TB_PEER, D_MODEL), last), pl.BlockSpec((TB_PEER, D_MODEL), last)],
        out_shape=[jax.ShapeDtypeStruct(h1.shape, F32), jax.ShapeDtypeStruct(h1b.shape, BF16)],
        input_output_aliases={n_in: 0, n_in + 1: 1},
        compiler_params=_params(("arbitrary",)),
        name="mix_out_tail",
    )(o, g, bv, ycv, h, *out_params, h1, h1b)


CAND_ROWS = TOPK + 7 * 8 + 8


def _extract_top(s, ids, count, break_ties):
    rank = jnp.full(s.shape, NOT_RANKED, F32)
    big = jnp.float32(NOT_RANKED)
    vals = []
    for p in range(count):
        m = jnp.max(s, axis=0, keepdims=True)
        if break_ties:
            hit = ids == jnp.min(jnp.where(s == m, ids, big), axis=0, keepdims=True)
        else:
            hit = s == m
        rank = jnp.where(hit, jnp.float32(p), rank)
        s = jnp.where(hit, -jnp.inf, s)
        vals.append(m)
    return vals, rank


def _all_distinct(rank, count):
    picked = jnp.sum(jnp.where(rank < float(count), 1.0, 0.0), axis=0, keepdims=True)
    return jnp.where(picked == float(count), 1.0, 0.0)


def _peer_route_kernel(x_ref, wqt_ref, keys_ref, n1_ref, e1_ref, r2_ref, e2_ref, qt_ref):
    qt_ref[...] = lax.dot_general(wqt_ref[...], x_ref[...], (((1,), (1,)), ((), ())),
                                  preferred_element_type=F32)
    refs = (qt_ref, keys_ref, n1_ref, e1_ref, r2_ref, e2_ref)
    for h in range(PEER_HEADS):
        ok = _route_head(h, *refs, break_ties=False)

        @pl.when(jnp.min(ok) < 0.5)
        def _():
            _route_head(h, *refs, break_ties=True)


def _route_head(h, qt_ref, keys_ref, n1_ref, e1_ref, r2_ref, e2_ref, break_ties):
    lanes = qt_ref.shape[1]
    key_ids = lax.broadcasted_iota(jnp.int32, (N_KEYS, lanes), 0).astype(F32)
    row16 = lax.broadcasted_iota(jnp.int32, (TOPK, lanes), 0).astype(F32)
    row8 = lax.broadcasted_iota(jnp.int32, (8, lanes), 0).astype(F32)
    cand_ids = jnp.concatenate([row16] + [row8 + float(TOPK * p) for p in range(1, 8)]
                               + [(row8 + 8.0) * float(TOPK)], axis=0)
    s1 = jnp.dot(keys_ref[2 * h], qt_ref[(2 * h) * HALF:(2 * h + 1) * HALF, :].astype(BF16),
                 preferred_element_type=F32)
    s2 = jnp.dot(keys_ref[2 * h + 1], qt_ref[(2 * h + 1) * HALF:(2 * h + 2) * HALF, :].astype(BF16),
                 preferred_element_type=F32)
    a_vals, rank1 = _extract_top(s1, key_ids, TOPK, break_ties)
    b_vals, rank2 = _extract_top(s2, key_ids, TOPK, break_ties)
    b = jnp.zeros((TOPK, lanes), F32)
    a_hi = jnp.zeros((8, lanes), F32)
    for p in range(TOPK):
        b = jnp.where(row16 == float(p), b_vals[p], b)
    for p in range(8):
        a_hi = jnp.where(row8 == float(p), a_vals[8 + p], a_hi)
    cand = jnp.concatenate([a_vals[0] + b] + [a_vals[p] + b[0:8, :] for p in range(1, 8)]
                           + [a_hi + b_vals[0]], axis=0)
    _, crank = _extract_top(cand, cand_ids, TOPK, break_ties)
    ok = _all_distinct(rank1, TOPK) * _all_distinct(rank2, TOPK) * _all_distinct(crank, TOPK)
    sel = crank < float(TOPK)
    top = a_vals[0] + b_vals[0]
    z = jnp.sum(jnp.where(sel, jnp.exp(cand - top), 0.0), axis=0, keepdims=True)
    self32 = jnp.where(sel, 1.0, 0.0)
    counts = [jnp.sum(self32[0:TOPK, :], axis=0, keepdims=True)]
    counts += [jnp.sum(self32[TOPK + 8 * (p - 1):TOPK + 8 * p, :], axis=0, keepdims=True) for p in range(1, 8)]
    tail = self32[TOPK + 56:TOPK + 64, :]
    counts += [jnp.sum(jnp.where(row8 == float(p), tail, 0.0), axis=0, keepdims=True) for p in range(8)]
    n1 = jnp.zeros((N_KEYS, lanes), F32)
    for p in range(TOPK):
        n1 = jnp.where(rank1 == float(p), counts[p], n1)
    n1_ref[h] = n1
    e1_ref[h] = jnp.exp(s1 - a_vals[0])
    r2_ref[h] = rank2.astype(BF16)
    e2_ref[h] = (jnp.exp(s2 - b_vals[0]) * (0.5 / z)).astype(BF16)
    return ok


def _peer_route(h1b, wqt, keys):
    n = h1b.shape[0]
    maps = pl.BlockSpec((PEER_HEADS, N_KEYS, TB_ROUTE), lambda i: (0, 0, i))
    f32_map = jax.ShapeDtypeStruct((PEER_HEADS, N_KEYS, n), F32)
    bf16_map = jax.ShapeDtypeStruct((PEER_HEADS, N_KEYS, n), BF16)
    return pl.pallas_call(
        _peer_route_kernel,
        grid=(n // TB_ROUTE,),
        in_specs=[pl.BlockSpec((TB_ROUTE, D_MODEL), lambda i: (i, 0)),
                  pl.BlockSpec((2 * PEER_HEADS * HALF, D_MODEL), lambda i: (0, 0)),
                  pl.BlockSpec((2 * PEER_HEADS, N_KEYS, HALF), lambda i: (0, 0, 0))],
        out_specs=[maps] * 4,
        out_shape=[f32_map, f32_map, bf16_map, bf16_map],
        scratch_shapes=[pltpu.VMEM((2 * PEER_HEADS * HALF, TB_ROUTE), F32)],
        compiler_params=_params(("parallel",)),
        name="peer_route",
    )(h1b, wqt, keys)


SLAB = 16


def _peer_dense_kernel(xb_ref, x_ref, u_ref, vt_ref, n1_ref, e1_ref, r2_ref, e2_ref, lg_ref, lb_ref,
                       y_ref, ytail_ref, acc_ref, ht_ref, act_ref):
    c = pl.program_id(1)

    @pl.when(c == 0)
    def _():
        acc_ref[...] = jnp.zeros_like(acc_ref)

    lanes = ht_ref.shape[1]
    zero = jnp.zeros((SLAB, lanes), BF16)
    xb = xb_ref[...]
    for sub in range(I_PER_CHUNK // I_PER_SUB):
        sub_rows = slice(sub * E_SUB, (sub + 1) * E_SUB)
        ht_ref[sub_rows, :] = lax.dot_general(u_ref[sub_rows, :], xb, (((1,), (1,)), ((), ())),
                                              preferred_element_type=F32)
    for sub in range(I_PER_CHUNK // I_PER_SUB):
        sub_rows = slice(sub * E_SUB, (sub + 1) * E_SUB)
        for ii in range(sub * I_PER_SUB, (sub + 1) * I_PER_SUB):
            n1 = [jnp.broadcast_to(n1_ref[h, 0, ii:ii + 1, :], (SLAB, lanes)).astype(BF16)
                  for h in range(PEER_HEADS)]
            e1 = [jnp.broadcast_to(e1_ref[h, 0, ii:ii + 1, :], (SLAB, lanes)).astype(BF16)
                  for h in range(PEER_HEADS)]
            for js in range(N_KEYS // SLAB):
                rows = slice(js * SLAB, (js + 1) * SLAB)
                gate = jnp.where(r2_ref[0, rows, :] < n1[0], e2_ref[0, rows, :], zero) * e1[0]
                for h in range(1, PEER_HEADS):
                    gate = gate + jnp.where(r2_ref[h, rows, :] < n1[h], e2_ref[h, rows, :], zero) * e1[h]
                out_rows = slice(ii * N_KEYS + js * SLAB, ii * N_KEYS + (js + 1) * SLAB)
                ht = ht_ref[out_rows, :]
                gelu2 = ht * (1.0 + lax.erf(ht * SQRT_HALF))
                act_ref[out_rows, :] = gelu2.astype(BF16) * gate
        acc_ref[...] += jnp.dot(vt_ref[:, sub_rows], act_ref[sub_rows, :], preferred_element_type=F32)

    is_tail = pl.program_id(0) == pl.num_programs(0) - 1

    @pl.when(c == pl.num_programs(1) - 1)
    def _():
        y = _layer_norm(ALPHA * x_ref[...] + acc_ref[...].T, lg_ref[...], lb_ref[...])

        @pl.when(jnp.logical_not(is_tail))
        def _():
            y_ref[...] = y

        @pl.when(is_tail)
        def _():
            ytail_ref[...] = y


def _peer_dense(h1b, h1, u, vt, n1, e1, r2, e2, lg, lb):
    n = h1b.shape[0]
    n_blocks = n // TB_PEER
    n_chunks = N_EXPERTS // E_CHUNK
    tok = pl.BlockSpec((TB_PEER, D_MODEL), lambda i, c: (i, 0))
    by_first = pl.BlockSpec((PEER_HEADS, 1, I_PER_CHUNK, TB_PEER), lambda i, c: (0, c, 0, i))
    by_second = pl.BlockSpec((PEER_HEADS, N_KEYS, TB_PEER), lambda i, c: (0, 0, i))
    vec = pl.BlockSpec((1, D_MODEL), lambda i, c: (0, 0))
    return pl.pallas_call(
        _peer_dense_kernel,
        grid=(n // TB_PEER, n_chunks),
        in_specs=[tok, tok,
                  pl.BlockSpec((E_CHUNK, D_MODEL), lambda i, c: (c, 0)),
                  pl.BlockSpec((D_MODEL, E_CHUNK), lambda i, c: (0, c)),
                  by_first, by_first, by_second, by_second, vec, vec],
        out_specs=[pl.BlockSpec((TB_PEER, D_MODEL), lambda i, c: (jnp.minimum(i, n_blocks - 2), 0)),
                   pl.BlockSpec((TB_PEER, D_MODEL), lambda i, c: (0, 0))],
        out_shape=[jax.ShapeDtypeStruct((n - TB_PEER, D_MODEL), F32),
                   jax.ShapeDtypeStruct((TB_PEER, D_MODEL), F32)],
        scratch_shapes=[pltpu.VMEM((D_MODEL, TB_PEER), F32), pltpu.VMEM((E_CHUNK, TB_PEER), F32),
                        pltpu.VMEM((E_CHUNK, TB_PEER), BF16)],
        compiler_params=_params(("arbitrary", "arbitrary")),
        name="peer_dense",
    )(h1b, h1, u, vt, n1, e1, r2, e2, lg, lb)


def _pad_cols(x, width):
    return jnp.pad(x, ((0, 0), (0, width - x.shape[1])))


def _pad_rows(x, height):
    return jnp.pad(x, ((0, height - x.shape[0]), (0, 0)))


def kernel(x_prompt, x_sample, state_conv, state_shift, state_wkv, meta_tokens, ln_in_g, ln_in_b, w_in, conv_w, mu, w0, w_up, a0, a_up, g_up, k_k, k_a, r_k, gn_g, gn_b, w_o, ln1_g, ln1_b, w_q, sub_keys, peer_u, peer_v, ln2_g, ln2_b):
    row = lambda x: x.reshape(1, -1)
    chan = np.arange(RWKV_WIDTH)
    perm = (chan % HEADS) * HEAD + chan // HEADS
    pc = lambda x: x[..., perm]
    w_in0 = w_in[0]
    wcv = w_in0[:, :CV_COLS].astype(BF16)
    rw0 = CV_COLS
    lo0 = rw0 + 3 * RWKV_WIDTH

    def rw_layout(x):
        groups = [pc(x[:, rw0 + i * RWKV_WIDTH:rw0 + (i + 1) * RWKV_WIDTH]) for i in range(3)]
        return jnp.concatenate(groups + [_pad_cols(x[:, lo0:lo0 + W_LORA], LORA_SLOT),
                                         _pad_cols(x[:, lo0 + W_LORA:lo0 + W_LORA + A_LORA], LORA_SLOT),
                                         _pad_cols(x[:, lo0 + W_LORA + A_LORA:], LORA_SLOT)], axis=1)

    wrw = rw_layout(w_in0).astype(BF16)
    mu_p = rw_layout(jnp.pad(mu, ((0, 0), (CV_COLS, 0))))
    ones_hd = jnp.asarray((chan[:, None] % HEADS) == (chan[None, :] % HEADS), BF16)
    lora = lambda w: _pad_rows(pc(w[0]), LORA_SLOT).astype(BF16)
    mix_params = (conv_w[0], mu_p, pc(w0), lora(w_up), pc(a0), lora(a_up), lora(g_up),
                  pc(k_k), pc(k_a), pc(r_k.reshape(1, RWKV_WIDTH)), ones_hd)
    out_params = (pc(gn_g), pc(gn_b), ones_hd, w_o[0, :CONV_WIDTH].astype(BF16),
                  w_o[0, CONV_WIDTH:][perm].astype(BF16), ln1_g, ln1_b)

    dt = x_prompt.dtype
    meta_blk = jnp.concatenate([jnp.zeros((TB_PROJ - N_META, D_MODEL), dt), meta_tokens.astype(dt)], axis=0)
    x_extra = jnp.stack([meta_blk, _pad_rows(x_sample.reshape(DEC_BATCH, D_MODEL), TB_PROJ),
                         jnp.zeros((TB_PROJ, D_MODEL), dt)])
    h, pcv, prw = _ln_proj(x_prompt, x_extra, row(ln_in_g), row(ln_in_b), wcv, wrw)
    prev_rw_s = _shift_proj(state_shift[0], wrw)

    outs_p = _mixer_seq(pcv, prw, jnp.zeros((BATCH, 8, CONV_WIDTH), F32), jnp.zeros((BATCH, 8, RW_PAD), F32),
                        mix_params)
    scan_p = outs_p[:N_SCAN_IN]
    ycv_p, g_p, bv_p = outs_p[N_SCAN_IN:N_SCAN_IN + N_TOK_OUT]
    conv_p = outs_p[N_SCAN_IN + N_TOK_OUT][:, 6:8]
    o_p, st_p = _wkv_seq(scan_p, jnp.zeros((V_ROWS, HEAD, LANES), F32))
    wkv_p = _seq_state_from_scan(st_p)
    h1_p, h1b_p = _mix_out_seq(o_p, g_p, bv_p, ycv_p, h, out_params)

    s_rows = slice(T_SEQS, T_SEQS + DEC_BATCH)
    outs_s = _mixer_rows(pcv[s_rows], prw[s_rows], prev_rw_s, state_conv[0, :, 1], state_conv[0, :, 0], mix_params)
    r_s, w_s, k_s, nkk_s, bb_s, v_s, ycv_s, g_s, bv_s, z_s = outs_s
    o_s, st_s = _wkv_scan(_to_scan_keys(r_s), _to_scan_keys(w_s), _to_scan_keys(k_s), _to_scan_keys(nkk_s),
                          _to_scan_keys(bb_s), _to_scan_vals(v_s), _state_to_scan(state_wkv[0]))
    wkv_s = _state_from_scan(st_s)
    tail = lambda x: _pad_rows(x, TB_PEER)
    h1, h1b = _mix_out_tail(tail(_from_scan_vals(o_s)), tail(g_s), tail(bv_s), tail(ycv_s), h, out_params,
                            h1_p, h1b_p)

    wqt = w_q[0].T.astype(BF16)
    keys = sub_keys[0].reshape(2 * PEER_HEADS, N_KEYS, HALF).astype(BF16)
    n1, e1, r2, e2 = _peer_route(h1b, wqt, keys)
    by_first = lambda x: x.reshape(PEER_HEADS, N_KEYS // I_PER_CHUNK, I_PER_CHUNK, T_PEER)
    y, y_tail = _peer_dense(h1b, h1, peer_u[0].astype(BF16), peer_v[0].T.astype(BF16), by_first(n1), by_first(e1),
                            r2, e2, row(ln2_g), row(ln2_b))

    y_prompt = y.reshape(BATCH, SEQ, D_MODEL)
    y_sample = y_tail[:DEC_BATCH].reshape(DEC_BATCH, 1, D_MODEL)
    shift_p = h[:T_SEQS].reshape(BATCH, S_PITCH, D_MODEL)[:, SEQ - 1]
    conv_s = jnp.stack([state_conv[0, :, 1], z_s], axis=1)
    return (y_prompt, y_sample, conv_p[None], shift_p[None], wkv_p[None],
            conv_s[None], h[s_rows][None], wkv_s[None])
```

```python
import functools

import numpy as np
import jax
import jax.numpy as jnp
from jax import lax
from jax.experimental import pallas as pl
from jax.experimental.pallas import tpu as pltpu

F32 = jnp.float32
BF16 = jnp.bfloat16

D_MODEL = 1024
N_META = 16
BATCH = 8
SEQ = 2048
DEC_BATCH = 128
T_BLK = 128
S_PITCH = 2304
N_TIME_BLK = SEQ // T_BLK + 1
META_BLK = S_PITCH // T_BLK - 1
T_SEQS = BATCH * S_PITCH
T_PHYS = T_SEQS + 512
T_REAL = BATCH * SEQ + DEC_BATCH
T_PEER = 16896
CONV_WIDTH = 512
RWKV_WIDTH = 512
HEAD = 64
HEADS = 8
W_LORA = 32
A_LORA = 32
G_LORA = 96
CV_COLS = 3 * CONV_WIDTH
LORA_SLOT = 128
RW_PAD = 3 * RWKV_WIDTH + 3 * LORA_SLOT
PEER_HEADS = 8
N_KEYS = 128
N_EXPERTS = N_KEYS * N_KEYS
TOPK = 16
HALF = 128
LN_EPS = 1e-5
GN_EPS = HEAD * 1e-5
ALPHA = 2.0 ** 0.25
SQRT_HALF = float(np.sqrt(0.5))
NOT_RANKED = 256.0
VMEM_LIMIT_BYTES = 56 * 1024 * 1024

TB_PROJ = 256
TB_MIX = 256
TB_OUT = 256
TB_ROUTE = 256
TB_PEER = 512
I_PER_CHUNK = 16
E_CHUNK = I_PER_CHUNK * N_KEYS
I_PER_SUB = 4
E_SUB = I_PER_SUB * N_KEYS


def _params(sem):
    return pltpu.CompilerParams(dimension_semantics=sem, vmem_limit_bytes=VMEM_LIMIT_BYTES)


def _layer_norm(x, g, b):
    m = jnp.mean(x, axis=-1, keepdims=True)
    xc = x - m
    var = jnp.mean(xc * xc, axis=-1, keepdims=True)
    return xc * lax.rsqrt(var + LN_EPS) * g + b


def _headsum(x, ones_hd):
    hi = x.astype(BF16)
    lo = (x - hi.astype(F32)).astype(BF16)
    return (jnp.dot(hi, ones_hd, preferred_element_type=F32)
            + jnp.dot(lo, ones_hd, preferred_element_type=F32))


def _time_block(l):
    return jnp.where(l == 0, META_BLK, l - 1)


S_CHAN = N_TIME_BLK * T_BLK


def _chan_block(l):
    return jnp.where(l == 0, N_TIME_BLK - 1, l - 1)


PROJ_PER_SEQ = S_PITCH // TB_PROJ
PROJ_REAL = SEQ // TB_PROJ
PROJ_SEQ_BLOCKS = BATCH * PROJ_PER_SEQ


def _ln_proj_kernel(xp_ref, xe_ref, g_ref, b_ref, wcv_ref, wrw_ref, h_ref, pcv_ref, prw_ref):
    k = pl.program_id(0)
    is_real = jnp.logical_and(k < PROJ_SEQ_BLOCKS, k % PROJ_PER_SEQ < PROJ_REAL)
    x = jnp.where(is_real, xp_ref[0], xe_ref[0])
    h = _layer_norm(x, g_ref[...], b_ref[...])
    h_ref[...] = h
    hb = h.astype(BF16)
    pcv_ref[...] = jnp.dot(hb, wcv_ref[...], preferred_element_type=F32)
    prw_ref[...] = jnp.dot(hb, wrw_ref[...], preferred_element_type=F32)


def _ln_proj(x_prompt, x_extra, g, b, wcv, wrw):
    n = T_PHYS
    row = lambda i: (i, 0)
    fixed = lambda i: (0, 0)
    prompt = lambda k: (jnp.minimum(k // PROJ_PER_SEQ, BATCH - 1), jnp.minimum(k % PROJ_PER_SEQ, PROJ_REAL - 1), 0)
    extra = lambda k: (jnp.maximum(k - PROJ_SEQ_BLOCKS + 1, 0), 0, 0)
    return pl.pallas_call(
        _ln_proj_kernel,
        grid=(n // TB_PROJ,),
        in_specs=[pl.BlockSpec((1, TB_PROJ, D_MODEL), prompt),
                  pl.BlockSpec((1, TB_PROJ, D_MODEL), extra),
                  pl.BlockSpec((1, D_MODEL), fixed),
                  pl.BlockSpec((1, D_MODEL), fixed),
                  pl.BlockSpec((D_MODEL, CV_COLS), fixed),
                  pl.BlockSpec((D_MODEL, RW_PAD), fixed)],
        out_specs=[pl.BlockSpec((TB_PROJ, D_MODEL), row),
                   pl.BlockSpec((TB_PROJ, CV_COLS), row),
                   pl.BlockSpec((TB_PROJ, RW_PAD), row)],
        out_shape=[jax.ShapeDtypeStruct((n, D_MODEL), F32),
                   jax.ShapeDtypeStruct((n, CV_COLS), F32),
                   jax.ShapeDtypeStruct((n, RW_PAD), F32)],
        compiler_params=_params(("parallel",)),
        name="ln_proj",
    )(x_prompt, x_extra, g, b, wcv, wrw)


def _shift_proj_kernel(x_ref, wrw_ref, prw_ref):
    prw_ref[...] = jnp.dot(x_ref[...].astype(BF16), wrw_ref[...], preferred_element_type=F32)


def _shift_proj(x, wrw):
    n = x.shape[0]
    return pl.pallas_call(
        _shift_proj_kernel,
        out_shape=jax.ShapeDtypeStruct((n, RW_PAD), F32),
        compiler_params=_params(None),
        name="shift_proj",
    )(x, wrw)


N_SCAN_IN = 6
N_TOK_OUT = 3


def _mixer_core(pcv, prw, prw_prev, z1, z2, cw, mu, w0, wup, a0, aup, gup, kkw, kaw, rkw, ones_hd):
    bg = pcv[:, 0:CONV_WIDTH]
    z = pcv[:, CONV_WIDTH:2 * CONV_WIDTH] * pcv[:, 2 * CONV_WIDTH:3 * CONV_WIDTH]
    conv = cw[0:1, :] * z2 + cw[1:2, :] * z1 + cw[2:3, :] * z
    y_cv = bg * conv

    m = prw + (prw_prev - prw) * mu
    r = m[:, 0:RWKV_WIDTH]
    k = m[:, RWKV_WIDTH:2 * RWKV_WIDTH]
    v = m[:, 2 * RWKV_WIDTH:3 * RWKV_WIDTH]
    base = 3 * RWKV_WIDTH
    wd = m[:, base:base + LORA_SLOT]
    ad = m[:, base + LORA_SLOT:base + 2 * LORA_SLOT]
    gd = m[:, base + 2 * LORA_SLOT:base + 3 * LORA_SLOT]

    xw = -(w0 + jnp.dot(jnp.tanh(wd).astype(BF16), wup, preferred_element_type=F32))
    softplus = jnp.maximum(xw, 0.0) + jnp.log1p(jnp.exp(-jnp.abs(xw)))
    w_log = -softplus - 0.5
    decay = jnp.exp(-jnp.exp(w_log))
    a = jax.nn.sigmoid(a0 + jnp.dot(ad.astype(BF16), aup, preferred_element_type=F32))
    g = jnp.dot(jax.nn.sigmoid(gd).astype(BF16), gup, preferred_element_type=F32)

    kk = k * kkw
    norm = jnp.sqrt(_headsum(kk * kk, ones_hd))
    kk = kk / jnp.maximum(norm, 1e-12)
    kx = k * (1.0 + (a - 1.0) * kaw)
    bonus_v = _headsum(r * kx * rkw, ones_hd) * v
    return z, (r, decay, kx, -kk, kk * a, v), (y_cv, g, bonus_v)


def _mixer_seq_kernel(pcv_ref, prw_ref, conv0_ref, prw0_ref, cw_ref, mu_ref, w0_ref, wup_ref, a0_ref, aup_ref,
                      gup_ref, kkw_ref, kaw_ref, rkw_ref, ones_ref, *rest):
    scan_refs = rest[:N_SCAN_IN]
    tok_refs = rest[N_SCAN_IN:N_SCAN_IN + N_TOK_OUT]
    convt_ref = rest[N_SCAN_IN + N_TOK_OUT]
    zc_ref, pc_ref = rest[N_SCAN_IN + N_TOK_OUT + 1:]
    l = pl.program_id(1)

    @pl.when(l == 0)
    def _():
        zc_ref[...] = conv0_ref[0]
        pc_ref[...] = prw0_ref[0]

    pcv = pcv_ref[...]
    prw = prw_ref[...]
    n = pcv.shape[0]
    first = jnp.where(l == 0, n - N_META, 0)
    z = pcv[:, CONV_WIDTH:2 * CONV_WIDTH] * pcv[:, 2 * CONV_WIDTH:3 * CONV_WIDTH]
    row_z = lax.broadcasted_iota(jnp.int32, z.shape, 0)
    row_p = lax.broadcasted_iota(jnp.int32, prw.shape, 0)
    c2 = zc_ref[6:7, :]
    c1 = zc_ref[7:8, :]
    z1 = jnp.where(row_z == first, c1, pltpu.roll(z, 1, axis=0))
    z2 = jnp.where(row_z == first, c2, jnp.where(row_z == first + 1, c1, pltpu.roll(z, 2, axis=0)))
    prw_prev = jnp.where(row_p == first, pc_ref[7:8, :], pltpu.roll(prw, 1, axis=0))

    _, scan_in, tok_out = _mixer_core(pcv, prw, prw_prev, z1, z2, cw_ref[...], mu_ref[...], w0_ref[...],
                                      wup_ref[...], a0_ref[...], aup_ref[...], gup_ref[...], kkw_ref[...],
                                      kaw_ref[...], rkw_ref[...], ones_ref[...])
    for ref, val in zip(scan_refs, scan_in):
        ref[0] = val.T
    for ref, val in zip(tok_refs, tok_out):
        ref[...] = val
    tail = z[n - 8:n, :]
    zc_ref[...] = tail
    pc_ref[...] = prw[n - 8:n, :]
    convt_ref[0] = tail


def _mixer_row_kernel(pcv_ref, prw_ref, prwprev_ref, z1_ref, z2_ref, cw_ref, mu_ref, w0_ref, wup_ref, a0_ref,
                      aup_ref, gup_ref, kkw_ref, kaw_ref, rkw_ref, ones_ref, *rest):
    out_refs = rest[:N_SCAN_IN + N_TOK_OUT]
    z_ref = rest[N_SCAN_IN + N_TOK_OUT]
    z, scan_in, tok_out = _mixer_core(pcv_ref[...], prw_ref[...], prwprev_ref[...], z1_ref[...], z2_ref[...],
                                      cw_ref[...], mu_ref[...], w0_ref[...], wup_ref[...], a0_ref[...],
                                      aup_ref[...], gup_ref[...], kkw_ref[...], kaw_ref[...], rkw_ref[...],
                                      ones_ref[...])
    for ref, val in zip(out_refs, scan_in + tok_out):
        ref[...] = val
    z_ref[...] = z


def _mixer_param_specs():
    fixed = (lambda *_: (0, 0))
    shapes = [(3, CONV_WIDTH), (1, RW_PAD), (1, RWKV_WIDTH), (LORA_SLOT, RWKV_WIDTH), (1, RWKV_WIDTH),
              (LORA_SLOT, RWKV_WIDTH), (LORA_SLOT, RWKV_WIDTH), (1, RWKV_WIDTH), (1, RWKV_WIDTH), (1, RWKV_WIDTH),
              (RWKV_WIDTH, RWKV_WIDTH)]
    return [pl.BlockSpec(s, fixed) for s in shapes]


def _mixer_seq(pcv, prw, conv0, prw0, mix_params):
    per_seq = S_PITCH // TB_MIX
    phys = lambda l: jnp.where(l == 0, per_seq - 1, l - 1)
    row = lambda b, l: (b * per_seq + phys(l), 0)
    chan = lambda b, l: (b, 0, phys(l))
    seq = lambda b, l: (b, 0, 0)
    return pl.pallas_call(
        _mixer_seq_kernel,
        grid=(BATCH, SEQ // TB_MIX + 1),
        in_specs=[pl.BlockSpec((TB_MIX, CV_COLS), row),
                  pl.BlockSpec((TB_MIX, RW_PAD), row),
                  pl.BlockSpec((1, 8, CONV_WIDTH), seq),
                  pl.BlockSpec((1, 8, RW_PAD), seq)] + _mixer_param_specs(),
        out_specs=[pl.BlockSpec((1, RWKV_WIDTH, TB_MIX), chan)] * N_SCAN_IN
                  + [pl.BlockSpec((TB_MIX, RWKV_WIDTH), row)] * N_TOK_OUT
                  + [pl.BlockSpec((1, 8, CONV_WIDTH), seq)],
        out_shape=[jax.ShapeDtypeStruct((BATCH, RWKV_WIDTH, S_PITCH), F32)] * N_SCAN_IN
                  + [jax.ShapeDtypeStruct((T_PHYS, RWKV_WIDTH), F32)] * N_TOK_OUT
                  + [jax.ShapeDtypeStruct((BATCH, 8, CONV_WIDTH), F32)],
        scratch_shapes=[pltpu.VMEM((8, CONV_WIDTH), F32), pltpu.VMEM((8, RW_PAD), F32)],
        compiler_params=_params(("arbitrary", "arbitrary")),
        name="mixer_seq",
    )(pcv, prw, conv0, prw0, *mix_params)


def _mixer_rows(pcv, prw, prw_prev, z1, z2, mix_params):
    n = pcv.shape[0]
    return pl.pallas_call(
        _mixer_row_kernel,
        out_shape=[jax.ShapeDtypeStruct((n, RWKV_WIDTH), F32)] * (N_SCAN_IN + N_TOK_OUT + 1),
        compiler_params=_params(None),
        name="mixer_rows",
    )(pcv, prw, prw_prev, z1, z2, *mix_params)


V_ROWS = HEAD // 2
LANES = 128
PITCH_K = T_BLK + 8
PITCH_O = V_ROWS + 8
RETILE_UNROLL = 4


def _wkv_step(s_scr, r, w, kx, nkk, bb, v_row, o_row):
    for vl in range(V_ROWS):
        sv = s_scr[vl]
        sa = jnp.sum(sv * nkk, axis=0, keepdims=True)
        sn = sv * w + sa * bb + v_row(vl) * kx
        s_scr[vl] = sn
        o_row(vl, jnp.sum(sn * r, axis=0, keepdims=True))


def _wkv_seq_kernel(r_ref, w_ref, k_ref, nkk_ref, bb_ref, v_ref, s0_ref, o_ref, st_ref,
                    s_scr, rs, ws, ks, ns, bs, vs, os_):
    l = pl.program_id(0)

    @pl.when(l == 0)
    def _():
        s_scr[...] = s0_ref[...]
        os_[...] = jnp.zeros_like(os_)

    def retile(src_ref, dst_ref, idx, second_idx):
        lo = pl.multiple_of(idx * 8, 8)
        hi = pl.multiple_of(second_idx * 8, 8)
        tile = jnp.concatenate([x for b in range(BATCH)
                                for x in (src_ref[b, pl.ds(lo, 8), :], src_ref[b, pl.ds(hi, 8), :])], axis=0)
        dst_ref[pl.ds(pl.multiple_of(idx * PITCH_K, 8), T_BLK), :] = tile.T

    def retile_keys(n, carry):
        for src, dst in ((r_ref, rs), (w_ref, ws), (k_ref, ks), (nkk_ref, ns), (bb_ref, bs)):
            retile(src, dst, n, n)
        return carry

    def retile_vals(vl, carry):
        retile(v_ref, vs, vl, vl + V_ROWS)
        return carry

    lax.fori_loop(0, HEAD, retile_keys, 0, unroll=RETILE_UNROLL)
    lax.fori_loop(0, V_ROWS, retile_vals, 0, unroll=RETILE_UNROLL)

    def step(t, carry):
        def keys(ref):
            return jnp.concatenate([ref[pl.ds(g * 8 * PITCH_K + t, 8, stride=PITCH_K), :]
                                    for g in range(HEAD // 8)], axis=0)

        def o_row(vl, val):
            os_[pl.ds(t * PITCH_O + vl, 1), :] = val

        _wkv_step(s_scr, keys(rs), keys(ws), keys(ks), keys(ns), keys(bs),
                  lambda vl: vs[pl.ds(vl * PITCH_K + t, 1), :], o_row)
        return carry

    lax.fori_loop(jnp.where(l == 0, T_BLK - N_META, 0), T_BLK, step, 0)

    def write_out(vl, carry):
        tile = os_[pl.ds(vl, T_BLK, stride=PITCH_O), :].T
        for b in range(BATCH):
            for hf in range(2):
                src = (2 * b + hf) * 8
                o_ref[b, pl.ds(pl.multiple_of((hf * V_ROWS + vl) * 8, 8), 8), :] = tile[src:src + 8, :]
        return carry

    lax.fori_loop(0, V_ROWS, write_out, 0, unroll=RETILE_UNROLL)

    @pl.when(l == pl.num_programs(0) - 1)
    def _():
        st_ref[...] = s_scr[...]


def _wkv_seq(scan_in, s0):
    blk = pl.BlockSpec((BATCH, RWKV_WIDTH, T_BLK), lambda l: (0, 0, _chan_block(l)))
    blk_in = pl.BlockSpec((BATCH, RWKV_WIDTH, T_BLK), lambda l: (0, 0, _time_block(l)),
                          pipeline_mode=pl.Buffered(1))
    st = pl.BlockSpec((V_ROWS, HEAD, LANES), lambda l: (0, 0, 0))
    retiled = pltpu.VMEM((HEAD * PITCH_K, LANES), F32)
    return pl.pallas_call(
        _wkv_seq_kernel,
        grid=(N_TIME_BLK,),
        in_specs=[blk_in] * N_SCAN_IN + [st],
        out_specs=[blk, st],
        out_shape=[jax.ShapeDtypeStruct((BATCH, RWKV_WIDTH, S_CHAN), F32),
                   jax.ShapeDtypeStruct((V_ROWS, HEAD, LANES), F32)],
        scratch_shapes=[pltpu.VMEM((V_ROWS, HEAD, LANES), F32)] + [retiled] * 5
                       + [pltpu.VMEM((V_ROWS * PITCH_K, LANES), F32), pltpu.VMEM((T_BLK * PITCH_O, LANES), F32)],
        compiler_params=_params(("arbitrary",)),
        name="wkv_seq",
    )(*scan_in, s0)


def _wkv_kernel(r_ref, w_ref, k_ref, nkk_ref, bb_ref, v_ref, s0_ref, o_ref, st_ref, s_scr):
    s_scr[...] = s0_ref[0]

    def o_row(vl, val):
        o_ref[0, pl.ds(vl, 1), :] = val

    _wkv_step(s_scr, r_ref[0], w_ref[0], k_ref[0], nkk_ref[0], bb_ref[0],
              lambda vl: v_ref[0, pl.ds(vl, 1), :], o_row)
    st_ref[0] = s_scr[...]


def _wkv_scan(r, w, k, nkk, bb, v, s0):
    groups = r.shape[0]
    key_spec = pl.BlockSpec((1, HEAD, LANES), lambda g: (g, 0, 0))
    val_spec = pl.BlockSpec((1, V_ROWS, LANES), lambda g: (g, 0, 0))
    st_spec = pl.BlockSpec((1, V_ROWS, HEAD, LANES), lambda g: (g, 0, 0, 0))
    return pl.pallas_call(
        _wkv_kernel,
        grid=(groups,),
        in_specs=[key_spec] * 5 + [val_spec, st_spec],
        out_specs=[val_spec, st_spec],
        out_shape=[jax.ShapeDtypeStruct((groups, V_ROWS, LANES), F32),
                   jax.ShapeDtypeStruct((groups, V_ROWS, HEAD, LANES), F32)],
        scratch_shapes=[pltpu.VMEM((V_ROWS, HEAD, LANES), F32)],
        compiler_params=_params(("parallel",)),
        name="wkv_scan",
    )(r, w, k, nkk, bb, v, s0)


def _to_scan_keys(x):
    g = x.shape[0] // BATCH
    x = x.reshape(g, BATCH, HEAD, HEADS).transpose(0, 2, 1, 3).reshape(g, HEAD, BATCH * HEADS)
    return jnp.concatenate([x, x], axis=-1)


def _to_scan_vals(x):
    g = x.shape[0] // BATCH
    return x.reshape(g, BATCH, 2, V_ROWS, HEADS).transpose(0, 3, 2, 1, 4).reshape(g, V_ROWS, LANES)


def _from_scan_vals(o):
    g = o.shape[0]
    return o.reshape(g, V_ROWS, 2, BATCH, HEADS).transpose(0, 3, 2, 1, 4).reshape(g * BATCH, RWKV_WIDTH)


def _state_to_scan(s):
    g = s.shape[0] // BATCH
    s = s.reshape(g, BATCH, HEADS, 2, V_ROWS, HEAD).transpose(0, 4, 5, 3, 1, 2)
    return s.reshape(g, V_ROWS, HEAD, LANES)


def _state_from_scan(s):
    g = s.shape[0]
    s = s.reshape(g, V_ROWS, HEAD, 2, BATCH, HEADS).transpose(0, 4, 5, 3, 1, 2)
    return s.reshape(g * BATCH, HEADS, HEAD, HEAD)


def _seq_state_from_scan(s):
    s = s.reshape(V_ROWS, HEAD, BATCH, 2, HEADS).transpose(2, 4, 3, 0, 1)
    return s.reshape(BATCH, HEADS, HEAD, HEAD)


def _mix_out_kernel(o_ref, g_ref, bv_ref, ycv_ref, h_ref, gng_ref, gnb_ref, ones_ref, wo1_ref, wo2_ref,
                    lg_ref, lb_ref, *rest, o_channel_major):
    h1_ref, h1b_ref = rest[-2:]
    ones_hd = ones_ref[...]
    o = o_ref[0].T if o_channel_major else o_ref[...]
    d = o - _headsum(o, ones_hd) * (1.0 / HEAD)
    var = _headsum(d * d, ones_hd) * (1.0 / HEAD)
    gn = d * lax.rsqrt(var + GN_EPS) * gng_ref[...] + gnb_ref[...]
    y_rw = (gn + bv_ref[...]) * g_ref[...]
    mix = (jnp.dot(ycv_ref[...].astype(BF16), wo1_ref[...], preferred_element_type=F32)
           + jnp.dot(y_rw.astype(BF16), wo2_ref[...], preferred_element_type=F32))
    h1 = _layer_norm(ALPHA * h_ref[...] + mix, lg_ref[...], lb_ref[...])
    h1_ref[...] = h1
    h1b_ref[...] = h1.astype(BF16)


def _mix_out_call(kernel, grid, o_spec, tok_map, out_map, n_out, tb, args):
    fixed = lambda *_: (0, 0)
    tok512 = pl.BlockSpec((tb, RWKV_WIDTH), tok_map)
    return pl.pallas_call(
        kernel,
        grid=grid,
        in_specs=[o_spec, tok512, tok512, tok512, pl.BlockSpec((tb, D_MODEL), tok_map),
                  pl.BlockSpec((1, RWKV_WIDTH), fixed), pl.BlockSpec((1, RWKV_WIDTH), fixed),
                  pl.BlockSpec((RWKV_WIDTH, RWKV_WIDTH), fixed),
                  pl.BlockSpec((CONV_WIDTH, D_MODEL), fixed), pl.BlockSpec((RWKV_WIDTH, D_MODEL), fixed),
                  pl.BlockSpec((1, D_MODEL), fixed), pl.BlockSpec((1, D_MODEL), fixed)],
        out_specs=[pl.BlockSpec((tb, D_MODEL), out_map), pl.BlockSpec((tb, D_MODEL), out_map)],
        out_shape=[jax.ShapeDtypeStruct((n_out, D_MODEL), F32), jax.ShapeDtypeStruct((n_out, D_MODEL), BF16)],
        compiler_params=_params(("parallel",) * len(grid)),
        name="mix_out",
    )(*args)


def _mix_out_seq(o, g, bv, ycv, h, out_params):
    per_seq = S_PITCH // TB_OUT
    real = SEQ // TB_OUT
    return _mix_out_call(functools.partial(_mix_out_kernel, o_channel_major=True), (BATCH, real),
                         pl.BlockSpec((1, RWKV_WIDTH, TB_OUT), lambda b, j: (b, 0, j)),
                         lambda b, j: (b * per_seq + j, 0), lambda b, j: (b * real + j, 0),
                         T_PEER, TB_OUT, (o, g, bv, ycv, h, *out_params))


def _mix_out_tail(o, g, bv, ycv, h, out_params, h1, h1b):
    tok = lambda i: (0, 0)
    fixed = lambda *_: (0, 0)
    last = lambda i: (T_PEER // TB_PEER - 1, 0)
    tok512 = pl.BlockSpec((TB_PEER, RWKV_WIDTH), tok)
    whole = pl.BlockSpec(memory_space=pl.ANY)
    n_in = 5 + len(out_params)
    return pl.pallas_call(
        functools.partial(_mix_out_kernel, o_channel_major=False),
        grid=(1,),
        in_specs=[tok512, tok512, tok512, tok512, pl.BlockSpec((TB_PEER, D_MODEL), lambda i: (T_SEQS // TB_PEER, 0)),
                  pl.BlockSpec((1, RWKV_WIDTH), fixed), pl.BlockSpec((1, RWKV_WIDTH), fixed),
                  pl.BlockSpec((RWKV_WIDTH, RWKV_WIDTH), fixed),
                  pl.BlockSpec((CONV_WIDTH, D_MODEL), fixed), pl.BlockSpec((RWKV_WIDTH, D_MODEL), fixed),
                  pl.BlockSpec((1, D_MODEL), fixed), pl.BlockSpec((1, D_MODEL), fixed), whole, whole],
        out_specs=[pl.BlockSpec((TB_PEER, D_MODEL), last), pl.BlockSpec((TB_PEER, D_MODEL), last)],
        out_shape=[jax.ShapeDtypeStruct(h1.shape, F32), jax.ShapeDtypeStruct(h1b.shape, BF16)],
        input_output_aliases={n_in: 0, n_in + 1: 1},
        compiler_params=_params(("arbitrary",)),
        name="mix_out_tail",
    )(o, g, bv, ycv, h, *out_params, h1, h1b)


CAND_ROWS = TOPK + 7 * 8 + 8
ROUTE_LANES = 128


def _extract_top(s, ids, count, break_ties):
    rank = jnp.full(s.shape, NOT_RANKED, F32)
    big = jnp.float32(NOT_RANKED)
    vals = []
    for p in range(count):
        m = jnp.max(s, axis=0, keepdims=True)
        if break_ties:
            hit = ids == jnp.min(jnp.where(s == m, ids, big), axis=0, keepdims=True)
        else:
            hit = s == m
        rank = jnp.where(hit, jnp.float32(p), rank)
        s = jnp.where(hit, -jnp.inf, s)
        vals.append(m)
    return vals, rank


def _all_distinct(rank, count):
    picked = jnp.sum(jnp.where(rank < float(count), 1.0, 0.0), axis=0, keepdims=True)
    return jnp.where(picked == float(count), 1.0, 0.0)


def _peer_route_kernel(x_ref, wqt_ref, keys_ref, n1_ref, e1_ref, r2_ref, e2_ref, qt_ref):
    qt_ref[...] = lax.dot_general(wqt_ref[...], x_ref[...], (((1,), (1,)), ((), ())),
                                  preferred_element_type=F32)
    refs = (qt_ref, keys_ref, n1_ref, e1_ref, r2_ref, e2_ref)
    tiles = [slice(t * ROUTE_LANES, (t + 1) * ROUTE_LANES) for t in range(x_ref.shape[0] // ROUTE_LANES)]
    for h in range(PEER_HEADS):
        oks = [_route_head(h, cols, *refs, break_ties=False) for cols in tiles]
        for cols, ok in zip(tiles, oks):
            @pl.when(jnp.min(ok) < 0.5)
            def _():
                _route_head(h, cols, *refs, break_ties=True)


def _route_head(h, cols, qt_ref, keys_ref, n1_ref, e1_ref, r2_ref, e2_ref, break_ties):
    lanes = ROUTE_LANES
    key_ids = lax.broadcasted_iota(jnp.int32, (N_KEYS, lanes), 0).astype(F32)
    row16 = lax.broadcasted_iota(jnp.int32, (TOPK, lanes), 0).astype(F32)
    row8 = lax.broadcasted_iota(jnp.int32, (8, lanes), 0).astype(F32)
    cand_ids = jnp.concatenate([row16] + [row8 + float(TOPK * p) for p in range(1, 8)]
                               + [(row8 + 8.0) * float(TOPK)], axis=0)
    s1 = jnp.dot(keys_ref[2 * h], qt_ref[(2 * h) * HALF:(2 * h + 1) * HALF, cols].astype(BF16),
                 preferred_element_type=F32)
    s2 = jnp.dot(keys_ref[2 * h + 1], qt_ref[(2 * h + 1) * HALF:(2 * h + 2) * HALF, cols].astype(BF16),
                 preferred_element_type=F32)
    a_vals, rank1 = _extract_top(s1, key_ids, TOPK, break_ties)
    b_vals, rank2 = _extract_top(s2, key_ids, TOPK, break_ties)
    b = jnp.zeros((TOPK, lanes), F32)
    a_hi = jnp.zeros((8, lanes), F32)
    for p in range(TOPK):
        b = jnp.where(row16 == float(p), b_vals[p], b)
    for p in range(8):
        a_hi = jnp.where(row8 == float(p), a_vals[8 + p], a_hi)
    cand = jnp.concatenate([a_vals[0] + b] + [a_vals[p] + b[0:8, :] for p in range(1, 8)]
                           + [a_hi + b_vals[0]], axis=0)
    _, crank = _extract_top(cand, cand_ids, TOPK, break_ties)
    ok = _all_distinct(rank1, TOPK) * _all_distinct(rank2, TOPK) * _all_distinct(crank, TOPK)
    sel = crank < float(TOPK)
    top = a_vals[0] + b_vals[0]
    z = jnp.sum(jnp.where(sel, jnp.exp(cand - top), 0.0), axis=0, keepdims=True)
    self32 = jnp.where(sel, 1.0, 0.0)
    counts = [jnp.sum(self32[0:TOPK, :], axis=0, keepdims=True)]
    counts += [jnp.sum(self32[TOPK + 8 * (p - 1):TOPK + 8 * p, :], axis=0, keepdims=True) for p in range(1, 8)]
    tail = self32[TOPK + 56:TOPK + 64, :]
    counts += [jnp.sum(jnp.where(row8 == float(p), tail, 0.0), axis=0, keepdims=True) for p in range(8)]
    n1 = jnp.zeros((N_KEYS, lanes), F32)
    for p in range(TOPK):
        n1 = jnp.where(rank1 == float(p), counts[p], n1)
    n1_ref[h, :, cols] = n1
    e1_ref[h, :, cols] = jnp.exp(s1 - a_vals[0])
    r2_ref[h, :, cols] = rank2.astype(BF16)
    e2_ref[h, :, cols] = (jnp.exp(s2 - b_vals[0]) * (0.5 / z)).astype(BF16)
    return ok


def _peer_route(h1b, wqt, keys):
    n = h1b.shape[0]
    maps = pl.BlockSpec((PEER_HEADS, N_KEYS, TB_ROUTE), lambda i: (0, 0, i))
    f32_map = jax.ShapeDtypeStruct((PEER_HEADS, N_KEYS, n), F32)
    bf16_map = jax.ShapeDtypeStruct((PEER_HEADS, N_KEYS, n), BF16)
    return pl.pallas_call(
        _peer_route_kernel,
        grid=(n // TB_ROUTE,),
        in_specs=[pl.BlockSpec((TB_ROUTE, D_MODEL), lambda i: (i, 0)),
                  pl.BlockSpec((2 * PEER_HEADS * HALF, D_MODEL), lambda i: (0, 0)),
                  pl.BlockSpec((2 * PEER_HEADS, N_KEYS, HALF), lambda i: (0, 0, 0))],
        out_specs=[maps] * 4,
        out_shape=[f32_map, f32_map, bf16_map, bf16_map],
        scratch_shapes=[pltpu.VMEM((2 * PEER_HEADS * HALF, TB_ROUTE), F32)],
        compiler_params=_params(("parallel",)),
        name="peer_route",
    )(h1b, wqt, keys)


SLAB = 16


def _peer_dense_kernel(xb_ref, x_ref, u_ref, vt_ref, n1_ref, e1_ref, r2_ref, e2_ref, lg_ref, lb_ref,
                       y_ref, ytail_ref, acc_ref, ht_ref, act_ref):
    c = pl.program_id(1)

    @pl.when(c == 0)
    def _():
        acc_ref[...] = jnp.zeros_like(acc_ref)

    lanes = ht_ref.shape[1]
    zero = jnp.zeros((SLAB, lanes), BF16)
    xb = xb_ref[...]
    for sub in range(I_PER_CHUNK // I_PER_SUB):
        sub_rows = slice(sub * E_SUB, (sub + 1) * E_SUB)
        ht_ref[sub_rows, :] = lax.dot_general(u_ref[sub_rows, :], xb, (((1,), (1,)), ((), ())),
                                              preferred_element_type=F32)
    for sub in range(I_PER_CHUNK // I_PER_SUB):
        sub_rows = slice(sub * E_SUB, (sub + 1) * E_SUB)
        for ii in range(sub * I_PER_SUB, (sub + 1) * I_PER_SUB):
            n1 = [jnp.broadcast_to(n1_ref[h, 0, ii:ii + 1, :], (SLAB, lanes)).astype(BF16)
                  for h in range(PEER_HEADS)]
            e1 = [jnp.broadcast_to(e1_ref[h, 0, ii:ii + 1, :], (SLAB, lanes)).astype(BF16)
                  for h in range(PEER_HEADS)]
            for js in range(N_KEYS // SLAB):
                rows = slice(js * SLAB, (js + 1) * SLAB)
                gate = jnp.where(r2_ref[0, rows, :] < n1[0], e2_ref[0, rows, :], zero) * e1[0]
                for h in range(1, PEER_HEADS):
                    gate = gate + jnp.where(r2_ref[h, rows, :] < n1[h], e2_ref[h, rows, :], zero) * e1[h]
                out_rows = slice(ii * N_KEYS + js * SLAB, ii * N_KEYS + (js + 1) * SLAB)
                ht = ht_ref[out_rows, :]
                gelu2 = ht * (1.0 + lax.erf(ht * SQRT_HALF))
                act_ref[out_rows, :] = gelu2.astype(BF16) * gate
        acc_ref[...] += jnp.dot(vt_ref[:, sub_rows], act_ref[sub_rows, :], preferred_element_type=F32)

    is_tail = pl.program_id(0) == pl.num_programs(0) - 1

    @pl.when(c == pl.num_programs(1) - 1)
    def _():
        y = _layer_norm(ALPHA * x_ref[...] + acc_ref[...].T, lg_ref[...], lb_ref[...])

        @pl.when(jnp.logical_not(is_tail))
        def _():
            y_ref[...] = y

        @pl.when(is_tail)
        def _():
            ytail_ref[...] = y


def _peer_dense(h1b, h1, u, vt, n1, e1, r2, e2, lg, lb):
    n = h1b.shape[0]
    n_blocks = n // TB_PEER
    n_chunks = N_EXPERTS // E_CHUNK
    tok = pl.BlockSpec((TB_PEER, D_MODEL), lambda i, c: (i, 0))
    by_first = pl.BlockSpec((PEER_HEADS, 1, I_PER_CHUNK, TB_PEER), lambda i, c: (0, c, 0, i))
    by_second = pl.BlockSpec((PEER_HEADS, N_KEYS, TB_PEER), lambda i, c: (0, 0, i))
    vec = pl.BlockSpec((1, D_MODEL), lambda i, c: (0, 0))
    return pl.pallas_call(
        _peer_dense_kernel,
        grid=(n // TB_PEER, n_chunks),
        in_specs=[tok, tok,
                  pl.BlockSpec((E_CHUNK, D_MODEL), lambda i, c: (c, 0)),
                  pl.BlockSpec((D_MODEL, E_CHUNK), lambda i, c: (0, c)),
                  by_first, by_first, by_second, by_second, vec, vec],
        out_specs=[pl.BlockSpec((TB_PEER, D_MODEL), lambda i, c: (jnp.minimum(i, n_blocks - 2), 0)),
                   pl.BlockSpec((TB_PEER, D_MODEL), lambda i, c: (0, 0))],
        out_shape=[jax.ShapeDtypeStruct((n - TB_PEER, D_MODEL), F32),
                   jax.ShapeDtypeStruct((TB_PEER, D_MODEL), F32)],
        scratch_shapes=[pltpu.VMEM((D_MODEL, TB_PEER), F32), pltpu.VMEM((E_CHUNK, TB_PEER), F32),
                        pltpu.VMEM((E_CHUNK, TB_PEER), BF16)],
        compiler_params=_params(("arbitrary", "arbitrary")),
        name="peer_dense",
    )(h1b, h1, u, vt, n1, e1, r2, e2, lg, lb)


def _pad_cols(x, width):
    return jnp.pad(x, ((0, 0), (0, width - x.shape[1])))


def _pad_rows(x, height):
    return jnp.pad(x, ((0, height - x.shape[0]), (0, 0)))


def kernel(x_prompt, x_sample, state_conv, state_shift, state_wkv, meta_tokens, ln_in_g, ln_in_b, w_in, conv_w, mu, w0, w_up, a0, a_up, g_up, k_k, k_a, r_k, gn_g, gn_b, w_o, ln1_g, ln1_b, w_q, sub_keys, peer_u, peer_v, ln2_g, ln2_b):
    row = lambda x: x.reshape(1, -1)
    chan = np.arange(RWKV_WIDTH)
    perm = (chan % HEADS) * HEAD + chan // HEADS
    pc = lambda x: x[..., perm]
    w_in0 = w_in[0]
    wcv = w_in0[:, :CV_COLS].astype(BF16)
    rw0 = CV_COLS
    lo0 = rw0 + 3 * RWKV_WIDTH

    def rw_layout(x):
        groups = [pc(x[:, rw0 + i * RWKV_WIDTH:rw0 + (i + 1) * RWKV_WIDTH]) for i in range(3)]
        return jnp.concatenate(groups + [_pad_cols(x[:, lo0:lo0 + W_LORA], LORA_SLOT),
                                         _pad_cols(x[:, lo0 + W_LORA:lo0 + W_LORA + A_LORA], LORA_SLOT),
                                         _pad_cols(x[:, lo0 + W_LORA + A_LORA:], LORA_SLOT)], axis=1)

    wrw = rw_layout(w_in0).astype(BF16)
    mu_p = rw_layout(jnp.pad(mu, ((0, 0), (CV_COLS, 0))))
    ones_hd = jnp.asarray((chan[:, None] % HEADS) == (chan[None, :] % HEADS), BF16)
    lora = lambda w: _pad_rows(pc(w[0]), LORA_SLOT).astype(BF16)
    mix_params = (conv_w[0], mu_p, pc(w0), lora(w_up), pc(a0), lora(a_up), lora(g_up),
                  pc(k_k), pc(k_a), pc(r_k.reshape(1, RWKV_WIDTH)), ones_hd)
    out_params = (pc(gn_g), pc(gn_b), ones_hd, w_o[0, :CONV_WIDTH].astype(BF16),
                  w_o[0, CONV_WIDTH:][perm].astype(BF16), ln1_g, ln1_b)

    dt = x_prompt.dtype
    meta_blk = jnp.concatenate([jnp.zeros((TB_PROJ - N_META, D_MODEL), dt), meta_tokens.astype(dt)], axis=0)
    x_extra = jnp.stack([meta_blk, _pad_rows(x_sample.reshape(DEC_BATCH, D_MODEL), TB_PROJ),
                         jnp.zeros((TB_PROJ, D_MODEL), dt)])
    h, pcv, prw = _ln_proj(x_prompt, x_extra, row(ln_in_g), row(ln_in_b), wcv, wrw)
    prev_rw_s = _shift_proj(state_shift[0], wrw)

    outs_p = _mixer_seq(pcv, prw, jnp.zeros((BATCH, 8, CONV_WIDTH), F32), jnp.zeros((BATCH, 8, RW_PAD), F32),
                        mix_params)
    scan_p = outs_p[:N_SCAN_IN]
    ycv_p, g_p, bv_p = outs_p[N_SCAN_IN:N_SCAN_IN + N_TOK_OUT]
    conv_p = outs_p[N_SCAN_IN + N_TOK_OUT][:, 6:8]
    o_p, st_p = _wkv_seq(scan_p, jnp.zeros((V_ROWS, HEAD, LANES), F32))
    wkv_p = _seq_state_from_scan(st_p)
    h1_p, h1b_p = _mix_out_seq(o_p, g_p, bv_p, ycv_p, h, out_params)

    s_rows = slice(T_SEQS, T_SEQS + DEC_BATCH)
    outs_s = _mixer_rows(pcv[s_rows], prw[s_rows], prev_rw_s, state_conv[0, :, 1], state_conv[0, :, 0], mix_params)
    r_s, w_s, k_s, nkk_s, bb_s, v_s, ycv_s, g_s, bv_s, z_s = outs_s
    o_s, st_s = _wkv_scan(_to_scan_keys(r_s), _to_scan_keys(w_s), _to_scan_keys(k_s), _to_scan_keys(nkk_s),
                          _to_scan_keys(bb_s), _to_scan_vals(v_s), _state_to_scan(state_wkv[0]))
    wkv_s = _state_from_scan(st_s)
    tail = lambda x: _pad_rows(x, TB_PEER)
    h1, h1b = _mix_out_tail(tail(_from_scan_vals(o_s)), tail(g_s), tail(bv_s), tail(ycv_s), h, out_params,
                            h1_p, h1b_p)

    wqt = w_q[0].T.astype(BF16)
    keys = sub_keys[0].reshape(2 * PEER_HEADS, N_KEYS, HALF).astype(BF16)
    n1, e1, r2, e2 = _peer_route(h1b, wqt, keys)
    by_first = lambda x: x.reshape(PEER_HEADS, N_KEYS // I_PER_CHUNK, I_PER_CHUNK, T_PEER)
    y, y_tail = _peer_dense(h1b, h1, peer_u[0].astype(BF16), peer_v[0].T.astype(BF16), by_first(n1), by_first(e1),
                            r2, e2, row(ln2_g), row(ln2_b))

    y_prompt = y.reshape(BATCH, SEQ, D_MODEL)
    y_sample = y_tail[:DEC_BATCH].reshape(DEC_BATCH, 1, D_MODEL)
    shift_p = h[np.arange(BATCH) * S_PITCH + SEQ - 1]
    conv_s = jnp.stack([state_conv[0, :, 1], z_s], axis=1)
    return (y_prompt, y_sample, conv_p[None], shift_p[None], wkv_p[None],
            conv_s[None], h[s_rows][None], wkv_s[None])
```

```python
import functools

import numpy as np
import jax
import jax.numpy as jnp
from jax import lax
from jax.experimental import pallas as pl
from jax.experimental.pallas import tpu as pltpu

F32 = jnp.float32
BF16 = jnp.bfloat16

D_MODEL = 1024
N_META = 16
BATCH = 8
SEQ = 2048
DEC_BATCH = 128
T_BLK = 128
S_PITCH = 2304
N_TIME_BLK = SEQ // T_BLK + 1
META_BLK = S_PITCH // T_BLK - 1
T_SEQS = BATCH * S_PITCH
T_PEER = 16896
CONV_WIDTH = 512
RWKV_WIDTH = 512
HEAD = 64
HEADS = 8
W_LORA = 32
A_LORA = 32
G_LORA = 96
CV_COLS = 3 * CONV_WIDTH
LORA_SLOT = 128
RW_PAD = 3 * RWKV_WIDTH + 3 * LORA_SLOT
PEER_HEADS = 8
N_KEYS = 128
N_EXPERTS = N_KEYS * N_KEYS
TOPK = 16
HALF = 128
LN_EPS = 1e-5
GN_EPS = HEAD * 1e-5
ALPHA = 2.0 ** 0.25
SQRT_HALF = float(np.sqrt(0.5))
NOT_RANKED = 256.0
VMEM_LIMIT_BYTES = 56 * 1024 * 1024

TB_PROJ = 256
TB_MIX = 256
TB_OUT = 256
TB_ROUTE = 256
TB_PEER = 512
I_PER_CHUNK = 16
E_CHUNK = I_PER_CHUNK * N_KEYS
I_PER_SUB = 4
E_SUB = I_PER_SUB * N_KEYS


def _params(sem):
    return pltpu.CompilerParams(dimension_semantics=sem, vmem_limit_bytes=VMEM_LIMIT_BYTES)


def _layer_norm(x, g, b):
    m = jnp.mean(x, axis=-1, keepdims=True)
    xc = x - m
    var = jnp.mean(xc * xc, axis=-1, keepdims=True)
    return xc * lax.rsqrt(var + LN_EPS) * g + b


def _headsum(x, ones_hd):
    hi = x.astype(BF16)
    lo = (x - hi.astype(F32)).astype(BF16)
    return (jnp.dot(hi, ones_hd, preferred_element_type=F32)
            + jnp.dot(lo, ones_hd, preferred_element_type=F32))


def _time_block(l):
    return jnp.where(l == 0, META_BLK, l - 1)


S_CHAN = N_TIME_BLK * T_BLK


def _chan_block(l):
    return jnp.where(l == 0, N_TIME_BLK - 1, l - 1)


def _ln_proj_kernel(x_ref, g_ref, b_ref, wcv_ref, wrw_ref, h_ref, pcv_ref, prw_ref):
    h = _layer_norm(x_ref[...], g_ref[...], b_ref[...])
    h_ref[...] = h
    hb = h.astype(BF16)
    pcv_ref[...] = jnp.dot(hb, wcv_ref[...], preferred_element_type=F32)
    prw_ref[...] = jnp.dot(hb, wrw_ref[...], preferred_element_type=F32)


def _ln_proj(x, g, b, wcv, wrw):
    n = x.shape[0]
    row = lambda i: (i, 0)
    fixed = lambda i: (0, 0)
    return pl.pallas_call(
        _ln_proj_kernel,
        grid=(n // TB_PROJ,),
        in_specs=[pl.BlockSpec((TB_PROJ, D_MODEL), row),
                  pl.BlockSpec((1, D_MODEL), fixed),
                  pl.BlockSpec((1, D_MODEL), fixed),
                  pl.BlockSpec((D_MODEL, CV_COLS), fixed),
                  pl.BlockSpec((D_MODEL, RW_PAD), fixed)],
        out_specs=[pl.BlockSpec((TB_PROJ, D_MODEL), row),
                   pl.BlockSpec((TB_PROJ, CV_COLS), row),
                   pl.BlockSpec((TB_PROJ, RW_PAD), row)],
        out_shape=[jax.ShapeDtypeStruct((n, D_MODEL), F32),
                   jax.ShapeDtypeStruct((n, CV_COLS), F32),
                   jax.ShapeDtypeStruct((n, RW_PAD), F32)],
        compiler_params=_params(("parallel",)),
        name="ln_proj",
    )(x, g, b, wcv, wrw)


def _shift_proj_kernel(x_ref, wrw_ref, prw_ref):
    prw_ref[...] = jnp.dot(x_ref[...].astype(BF16), wrw_ref[...], preferred_element_type=F32)


def _shift_proj(x, wrw):
    n = x.shape[0]
    return pl.pallas_call(
        _shift_proj_kernel,
        out_shape=jax.ShapeDtypeStruct((n, RW_PAD), F32),
        compiler_params=_params(None),
        name="shift_proj",
    )(x, wrw)


N_SCAN_IN = 6
N_TOK_OUT = 3


def _mixer_core(pcv, prw, prw_prev, z1, z2, cw, mu, w0, wup, a0, aup, gup, kkw, kaw, rkw, ones_hd):
    bg = pcv[:, 0:CONV_WIDTH]
    z = pcv[:, CONV_WIDTH:2 * CONV_WIDTH] * pcv[:, 2 * CONV_WIDTH:3 * CONV_WIDTH]
    conv = cw[0:1, :] * z2 + cw[1:2, :] * z1 + cw[2:3, :] * z
    y_cv = bg * conv

    m = prw + (prw_prev - prw) * mu
    r = m[:, 0:RWKV_WIDTH]
    k = m[:, RWKV_WIDTH:2 * RWKV_WIDTH]
    v = m[:, 2 * RWKV_WIDTH:3 * RWKV_WIDTH]
    base = 3 * RWKV_WIDTH
    wd = m[:, base:base + LORA_SLOT]
    ad = m[:, base + LORA_SLOT:base + 2 * LORA_SLOT]
    gd = m[:, base + 2 * LORA_SLOT:base + 3 * LORA_SLOT]

    xw = -(w0 + jnp.dot(jnp.tanh(wd).astype(BF16), wup, preferred_element_type=F32))
    softplus = jnp.maximum(xw, 0.0) + jnp.log1p(jnp.exp(-jnp.abs(xw)))
    w_log = -softplus - 0.5
    decay = jnp.exp(-jnp.exp(w_log))
    a = jax.nn.sigmoid(a0 + jnp.dot(ad.astype(BF16), aup, preferred_element_type=F32))
    g = jnp.dot(jax.nn.sigmoid(gd).astype(BF16), gup, preferred_element_type=F32)

    kk = k * kkw
    norm = jnp.sqrt(_headsum(kk * kk, ones_hd))
    kk = kk / jnp.maximum(norm, 1e-12)
    kx = k * (1.0 + (a - 1.0) * kaw)
    bonus_v = _headsum(r * kx * rkw, ones_hd) * v
    return z, (r, decay, kx, -kk, kk * a, v), (y_cv, g, bonus_v)


def _mixer_seq_kernel(xp_ref, xe_ref, lng_ref, lnb_ref, wcv_ref, wrw_ref, conv0_ref, prw0_ref, cw_ref, mu_ref,
                      w0_ref, wup_ref, a0_ref, aup_ref, gup_ref, kkw_ref, kaw_ref, rkw_ref, ones_ref, *rest):
    scan_refs = rest[:N_SCAN_IN]
    tok_refs = rest[N_SCAN_IN:N_SCAN_IN + N_TOK_OUT]
    h_ref, convt_ref = rest[N_SCAN_IN + N_TOK_OUT:N_SCAN_IN + N_TOK_OUT + 2]
    zc_ref, pc_ref = rest[N_SCAN_IN + N_TOK_OUT + 2:]
    l = pl.program_id(1)

    @pl.when(l == 0)
    def _():
        zc_ref[...] = conv0_ref[0]
        pc_ref[...] = prw0_ref[0]

    h = _layer_norm(jnp.where(l == 0, xe_ref[0], xp_ref[0]), lng_ref[...], lnb_ref[...])
    h_ref[...] = h
    hb = h.astype(BF16)
    pcv = jnp.dot(hb, wcv_ref[...], preferred_element_type=F32)
    prw = jnp.dot(hb, wrw_ref[...], preferred_element_type=F32)
    n = pcv.shape[0]
    first = jnp.where(l == 0, n - N_META, 0)
    z = pcv[:, CONV_WIDTH:2 * CONV_WIDTH] * pcv[:, 2 * CONV_WIDTH:3 * CONV_WIDTH]
    row_z = lax.broadcasted_iota(jnp.int32, z.shape, 0)
    row_p = lax.broadcasted_iota(jnp.int32, prw.shape, 0)
    c2 = zc_ref[6:7, :]
    c1 = zc_ref[7:8, :]
    z1 = jnp.where(row_z == first, c1, pltpu.roll(z, 1, axis=0))
    z2 = jnp.where(row_z == first, c2, jnp.where(row_z == first + 1, c1, pltpu.roll(z, 2, axis=0)))
    prw_prev = jnp.where(row_p == first, pc_ref[7:8, :], pltpu.roll(prw, 1, axis=0))

    _, scan_in, tok_out = _mixer_core(pcv, prw, prw_prev, z1, z2, cw_ref[...], mu_ref[...], w0_ref[...],
                                      wup_ref[...], a0_ref[...], aup_ref[...], gup_ref[...], kkw_ref[...],
                                      kaw_ref[...], rkw_ref[...], ones_ref[...])
    for ref, val in zip(scan_refs, scan_in):
        ref[0] = val.T
    for ref, val in zip(tok_refs, tok_out):
        ref[...] = val
    tail = z[n - 8:n, :]
    zc_ref[...] = tail
    pc_ref[...] = prw[n - 8:n, :]
    convt_ref[0] = tail


def _mixer_row_kernel(pcv_ref, prw_ref, prwprev_ref, z1_ref, z2_ref, cw_ref, mu_ref, w0_ref, wup_ref, a0_ref,
                      aup_ref, gup_ref, kkw_ref, kaw_ref, rkw_ref, ones_ref, *rest):
    out_refs = rest[:N_SCAN_IN + N_TOK_OUT]
    z_ref = rest[N_SCAN_IN + N_TOK_OUT]
    z, scan_in, tok_out = _mixer_core(pcv_ref[...], prw_ref[...], prwprev_ref[...], z1_ref[...], z2_ref[...],
                                      cw_ref[...], mu_ref[...], w0_ref[...], wup_ref[...], a0_ref[...],
                                      aup_ref[...], gup_ref[...], kkw_ref[...], kaw_ref[...], rkw_ref[...],
                                      ones_ref[...])
    for ref, val in zip(out_refs, scan_in + tok_out):
        ref[...] = val
    z_ref[...] = z


def _mixer_param_specs():
    fixed = (lambda *_: (0, 0))
    shapes = [(3, CONV_WIDTH), (1, RW_PAD), (1, RWKV_WIDTH), (LORA_SLOT, RWKV_WIDTH), (1, RWKV_WIDTH),
              (LORA_SLOT, RWKV_WIDTH), (LORA_SLOT, RWKV_WIDTH), (1, RWKV_WIDTH), (1, RWKV_WIDTH), (1, RWKV_WIDTH),
              (RWKV_WIDTH, RWKV_WIDTH)]
    return [pl.BlockSpec(s, fixed) for s in shapes]


def _mixer_seq(x_prompt, x_extra, ln_g, ln_b, wcv, wrw, conv0, prw0, mix_params):
    per_seq = S_PITCH // TB_MIX
    phys = lambda l: jnp.where(l == 0, per_seq - 1, l - 1)
    row = lambda b, l: (b * per_seq + phys(l), 0)
    chan = lambda b, l: (b, 0, phys(l))
    seq = lambda b, l: (b, 0, 0)
    fixed2 = lambda b, l: (0, 0)
    return pl.pallas_call(
        _mixer_seq_kernel,
        grid=(BATCH, SEQ // TB_MIX + 1),
        in_specs=[pl.BlockSpec((1, TB_MIX, D_MODEL), lambda b, l: (b, jnp.maximum(l - 1, 0), 0)),
                  pl.BlockSpec((1, TB_MIX, D_MODEL), lambda b, l: (0, 0, 0)),
                  pl.BlockSpec((1, D_MODEL), fixed2), pl.BlockSpec((1, D_MODEL), fixed2),
                  pl.BlockSpec((D_MODEL, CV_COLS), fixed2), pl.BlockSpec((D_MODEL, RW_PAD), fixed2),
                  pl.BlockSpec((1, 8, CONV_WIDTH), seq),
                  pl.BlockSpec((1, 8, RW_PAD), seq)] + _mixer_param_specs(),
        out_specs=[pl.BlockSpec((1, RWKV_WIDTH, TB_MIX), chan)] * N_SCAN_IN
                  + [pl.BlockSpec((TB_MIX, RWKV_WIDTH), row)] * N_TOK_OUT
                  + [pl.BlockSpec((TB_MIX, D_MODEL), row), pl.BlockSpec((1, 8, CONV_WIDTH), seq)],
        out_shape=[jax.ShapeDtypeStruct((BATCH, RWKV_WIDTH, S_PITCH), F32)] * N_SCAN_IN
                  + [jax.ShapeDtypeStruct((T_SEQS, RWKV_WIDTH), F32)] * N_TOK_OUT
                  + [jax.ShapeDtypeStruct((T_SEQS, D_MODEL), F32),
                     jax.ShapeDtypeStruct((BATCH, 8, CONV_WIDTH), F32)],
        scratch_shapes=[pltpu.VMEM((8, CONV_WIDTH), F32), pltpu.VMEM((8, RW_PAD), F32)],
        compiler_params=_params(("arbitrary", "arbitrary")),
        name="mixer_seq",
    )(x_prompt, x_extra, ln_g, ln_b, wcv, wrw, conv0, prw0, *mix_params)


def _mixer_rows(pcv, prw, prw_prev, z1, z2, mix_params):
    n = pcv.shape[0]
    return pl.pallas_call(
        _mixer_row_kernel,
        out_shape=[jax.ShapeDtypeStruct((n, RWKV_WIDTH), F32)] * (N_SCAN_IN + N_TOK_OUT + 1),
        compiler_params=_params(None),
        name="mixer_rows",
    )(pcv, prw, prw_prev, z1, z2, *mix_params)


V_ROWS = HEAD // 2
LANES = 128
PITCH_K = T_BLK + 8
PITCH_O = V_ROWS + 8
RETILE_UNROLL = 4


def _wkv_step(s_scr, r, w, kx, nkk, bb, v_row, o_row):
    for vl in range(V_ROWS):
        sv = s_scr[vl]
        sa = jnp.sum(sv * nkk, axis=0, keepdims=True)
        sn = sv * w + sa * bb + v_row(vl) * kx
        s_scr[vl] = sn
        o_row(vl, jnp.sum(sn * r, axis=0, keepdims=True))


def _wkv_seq_kernel(r_ref, w_ref, k_ref, nkk_ref, bb_ref, v_ref, s0_ref, o_ref, st_ref,
                    s_scr, rs, ws, ks, ns, bs, vs, os_):
    l = pl.program_id(0)

    @pl.when(l == 0)
    def _():
        s_scr[...] = s0_ref[...]
        os_[...] = jnp.zeros_like(os_)

    def retile(src_ref, dst_ref, idx, second_idx):
        lo = pl.multiple_of(idx * 8, 8)
        hi = pl.multiple_of(second_idx * 8, 8)
        tile = jnp.concatenate([x for b in range(BATCH)
                                for x in (src_ref[b, pl.ds(lo, 8), :], src_ref[b, pl.ds(hi, 8), :])], axis=0)
        dst_ref[pl.ds(pl.multiple_of(idx * PITCH_K, 8), T_BLK), :] = tile.T

    def retile_keys(n, carry):
        for src, dst in ((r_ref, rs), (w_ref, ws), (k_ref, ks), (nkk_ref, ns), (bb_ref, bs)):
            retile(src, dst, n, n)
        return carry

    def retile_vals(vl, carry):
        retile(v_ref, vs, vl, vl + V_ROWS)
        return carry

    lax.fori_loop(0, HEAD, retile_keys, 0, unroll=RETILE_UNROLL)
    lax.fori_loop(0, V_ROWS, retile_vals, 0, unroll=RETILE_UNROLL)

    def step(t, carry):
        def keys(ref):
            return jnp.concatenate([ref[pl.ds(g * 8 * PITCH_K + t, 8, stride=PITCH_K), :]
                                    for g in range(HEAD // 8)], axis=0)

        def o_row(vl, val):
            os_[pl.ds(t * PITCH_O + vl, 1), :] = val

        _wkv_step(s_scr, keys(rs), keys(ws), keys(ks), keys(ns), keys(bs),
                  lambda vl: vs[pl.ds(vl * PITCH_K + t, 1), :], o_row)
        return carry

    lax.fori_loop(jnp.where(l == 0, T_BLK - N_META, 0), T_BLK, step, 0)

    def write_out(vl, carry):
        tile = os_[pl.ds(vl, T_BLK, stride=PITCH_O), :].T
        for b in range(BATCH):
            for hf in range(2):
                src = (2 * b + hf) * 8
                o_ref[b, pl.ds(pl.multiple_of((hf * V_ROWS + vl) * 8, 8), 8), :] = tile[src:src + 8, :]
        return carry

    lax.fori_loop(0, V_ROWS, write_out, 0, unroll=RETILE_UNROLL)

    @pl.when(l == pl.num_programs(0) - 1)
    def _():
        st_ref[...] = s_scr[...]


def _wkv_seq(scan_in, s0):
    blk = pl.BlockSpec((BATCH, RWKV_WIDTH, T_BLK), lambda l: (0, 0, _chan_block(l)))
    blk_in = pl.BlockSpec((BATCH, RWKV_WIDTH, T_BLK), lambda l: (0, 0, _time_block(l)),
                          pipeline_mode=pl.Buffered(1))
    st = pl.BlockSpec((V_ROWS, HEAD, LANES), lambda l: (0, 0, 0))
    retiled = pltpu.VMEM((HEAD * PITCH_K, LANES), F32)
    return pl.pallas_call(
        _wkv_seq_kernel,
        grid=(N_TIME_BLK,),
        in_specs=[blk_in] * N_SCAN_IN + [st],
        out_specs=[blk, st],
        out_shape=[jax.ShapeDtypeStruct((BATCH, RWKV_WIDTH, S_CHAN), F32),
                   jax.ShapeDtypeStruct((V_ROWS, HEAD, LANES), F32)],
        scratch_shapes=[pltpu.VMEM((V_ROWS, HEAD, LANES), F32)] + [retiled] * 5
                       + [pltpu.VMEM((V_ROWS * PITCH_K, LANES), F32), pltpu.VMEM((T_BLK * PITCH_O, LANES), F32)],
        compiler_params=_params(("arbitrary",)),
        name="wkv_seq",
    )(*scan_in, s0)


def _wkv_kernel(r_ref, w_ref, k_ref, nkk_ref, bb_ref, v_ref, s0_ref, o_ref, st_ref, s_scr):
    s_scr[...] = s0_ref[0]

    def o_row(vl, val):
        o_ref[0, pl.ds(vl, 1), :] = val

    _wkv_step(s_scr, r_ref[0], w_ref[0], k_ref[0], nkk_ref[0], bb_ref[0],
              lambda vl: v_ref[0, pl.ds(vl, 1), :], o_row)
    st_ref[0] = s_scr[...]


def _wkv_scan(r, w, k, nkk, bb, v, s0):
    groups = r.shape[0]
    key_spec = pl.BlockSpec((1, HEAD, LANES), lambda g: (g, 0, 0))
    val_spec = pl.BlockSpec((1, V_ROWS, LANES), lambda g: (g, 0, 0))
    st_spec = pl.BlockSpec((1, V_ROWS, HEAD, LANES), lambda g: (g, 0, 0, 0))
    return pl.pallas_call(
        _wkv_kernel,
        grid=(groups,),
        in_specs=[key_spec] * 5 + [val_spec, st_spec],
        out_specs=[val_spec, st_spec],
        out_shape=[jax.ShapeDtypeStruct((groups, V_ROWS, LANES), F32),
                   jax.ShapeDtypeStruct((groups, V_ROWS, HEAD, LANES), F32)],
        scratch_shapes=[pltpu.VMEM((V_ROWS, HEAD, LANES), F32)],
        compiler_params=_params(("parallel",)),
        name="wkv_scan",
    )(r, w, k, nkk, bb, v, s0)


def _to_scan_keys(x):
    g = x.shape[0] // BATCH
    x = x.reshape(g, BATCH, HEAD, HEADS).transpose(0, 2, 1, 3).reshape(g, HEAD, BATCH * HEADS)
    return jnp.concatenate([x, x], axis=-1)


def _to_scan_vals(x):
    g = x.shape[0] // BATCH
    return x.reshape(g, BATCH, 2, V_ROWS, HEADS).transpose(0, 3, 2, 1, 4).reshape(g, V_ROWS, LANES)


def _from_scan_vals(o):
    g = o.shape[0]
    return o.reshape(g, V_ROWS, 2, BATCH, HEADS).transpose(0, 3, 2, 1, 4).reshape(g * BATCH, RWKV_WIDTH)


def _state_to_scan(s):
    g = s.shape[0] // BATCH
    s = s.reshape(g, BATCH, HEADS, 2, V_ROWS, HEAD).transpose(0, 4, 5, 3, 1, 2)
    return s.reshape(g, V_ROWS, HEAD, LANES)


def _state_from_scan(s):
    g = s.shape[0]
    s = s.reshape(g, V_ROWS, HEAD, 2, BATCH, HEADS).transpose(0, 4, 5, 3, 1, 2)
    return s.reshape(g * BATCH, HEADS, HEAD, HEAD)


def _seq_state_from_scan(s):
    s = s.reshape(V_ROWS, HEAD, BATCH, 2, HEADS).transpose(2, 4, 3, 0, 1)
    return s.reshape(BATCH, HEADS, HEAD, HEAD)


def _mix_out_kernel(o_ref, g_ref, bv_ref, ycv_ref, h_ref, gng_ref, gnb_ref, ones_ref, wo1_ref, wo2_ref,
                    lg_ref, lb_ref, *rest, o_channel_major):
    h1_ref, h1b_ref = rest[-2:]
    ones_hd = ones_ref[...]
    o = o_ref[0].T if o_channel_major else o_ref[...]
    d = o - _headsum(o, ones_hd) * (1.0 / HEAD)
    var = _headsum(d * d, ones_hd) * (1.0 / HEAD)
    gn = d * lax.rsqrt(var + GN_EPS) * gng_ref[...] + gnb_ref[...]
    y_rw = (gn + bv_ref[...]) * g_ref[...]
    mix = (jnp.dot(ycv_ref[...].astype(BF16), wo1_ref[...], preferred_element_type=F32)
           + jnp.dot(y_rw.astype(BF16), wo2_ref[...], preferred_element_type=F32))
    h1 = _layer_norm(ALPHA * h_ref[...] + mix, lg_ref[...], lb_ref[...])
    h1_ref[...] = h1
    h1b_ref[...] = h1.astype(BF16)


def _mix_out_call(kernel, grid, o_spec, tok_map, out_map, n_out, tb, args):
    fixed = lambda *_: (0, 0)
    tok512 = pl.BlockSpec((tb, RWKV_WIDTH), tok_map)
    return pl.pallas_call(
        kernel,
        grid=grid,
        in_specs=[o_spec, tok512, tok512, tok512, pl.BlockSpec((tb, D_MODEL), tok_map),
                  pl.BlockSpec((1, RWKV_WIDTH), fixed), pl.BlockSpec((1, RWKV_WIDTH), fixed),
                  pl.BlockSpec((RWKV_WIDTH, RWKV_WIDTH), fixed),
                  pl.BlockSpec((CONV_WIDTH, D_MODEL), fixed), pl.BlockSpec((RWKV_WIDTH, D_MODEL), fixed),
                  pl.BlockSpec((1, D_MODEL), fixed), pl.BlockSpec((1, D_MODEL), fixed)],
        out_specs=[pl.BlockSpec((tb, D_MODEL), out_map), pl.BlockSpec((tb, D_MODEL), out_map)],
        out_shape=[jax.ShapeDtypeStruct((n_out, D_MODEL), F32), jax.ShapeDtypeStruct((n_out, D_MODEL), BF16)],
        compiler_params=_params(("parallel",) * len(grid)),
        name="mix_out",
    )(*args)


def _mix_out_seq(o, g, bv, ycv, h, out_params):
    per_seq = S_PITCH // TB_OUT
    real = SEQ // TB_OUT
    return _mix_out_call(functools.partial(_mix_out_kernel, o_channel_major=True), (BATCH, real),
                         pl.BlockSpec((1, RWKV_WIDTH, TB_OUT), lambda b, j: (b, 0, j)),
                         lambda b, j: (b * per_seq + j, 0), lambda b, j: (b * real + j, 0),
                         T_PEER, TB_OUT, (o, g, bv, ycv, h, *out_params))


def _mix_out_tail(o, g, bv, ycv, h, out_params, h1, h1b):
    tok = lambda i: (0, 0)
    fixed = lambda *_: (0, 0)
    last = lambda i: (T_PEER // TB_PEER - 1, 0)
    tok512 = pl.BlockSpec((TB_PEER, RWKV_WIDTH), tok)
    whole = pl.BlockSpec(memory_space=pl.ANY)
    n_in = 5 + len(out_params)
    return pl.pallas_call(
        functools.partial(_mix_out_kernel, o_channel_major=False),
        grid=(1,),
        in_specs=[tok512, tok512, tok512, tok512, pl.BlockSpec((TB_PEER, D_MODEL), tok),
                  pl.BlockSpec((1, RWKV_WIDTH), fixed), pl.BlockSpec((1, RWKV_WIDTH), fixed),
                  pl.BlockSpec((RWKV_WIDTH, RWKV_WIDTH), fixed),
                  pl.BlockSpec((CONV_WIDTH, D_MODEL), fixed), pl.BlockSpec((RWKV_WIDTH, D_MODEL), fixed),
                  pl.BlockSpec((1, D_MODEL), fixed), pl.BlockSpec((1, D_MODEL), fixed), whole, whole],
        out_specs=[pl.BlockSpec((TB_PEER, D_MODEL), last), pl.BlockSpec((TB_PEER, D_MODEL), last)],
        out_shape=[jax.ShapeDtypeStruct(h1.shape, F32), jax.ShapeDtypeStruct(h1b.shape, BF16)],
        input_output_aliases={n_in: 0, n_in + 1: 1},
        compiler_params=_params(("arbitrary",)),
        name="mix_out_tail",
    )(o, g, bv, ycv, h, *out_params, h1, h1b)


CAND_ROWS = TOPK + 7 * 8 + 8
ROUTE_LANES = 128


def _extract_top(s, ids, count, break_ties):
    rank = jnp.full(s.shape, NOT_RANKED, F32)
    big = jnp.float32(NOT_RANKED)
    vals = []
    for p in range(count):
        m = jnp.max(s, axis=0, keepdims=True)
        if break_ties:
            hit = ids == jnp.min(jnp.where(s == m, ids, big), axis=0, keepdims=True)
        else:
            hit = s == m
        rank = jnp.where(hit, jnp.float32(p), rank)
        s = jnp.where(hit, -jnp.inf, s)
        vals.append(m)
    return vals, rank


def _all_distinct(rank, count):
    picked = jnp.sum(jnp.where(rank < float(count), 1.0, 0.0), axis=0, keepdims=True)
    return jnp.where(picked == float(count), 1.0, 0.0)


def _peer_route_kernel(x_ref, wqt_ref, keys_ref, n1_ref, e1_ref, r2_ref, e2_ref, qt_ref):
    qt_ref[...] = lax.dot_general(wqt_ref[...], x_ref[...], (((1,), (1,)), ((), ())),
                                  preferred_element_type=F32)
    refs = (qt_ref, keys_ref, n1_ref, e1_ref, r2_ref, e2_ref)
    tiles = [slice(t * ROUTE_LANES, (t + 1) * ROUTE_LANES) for t in range(x_ref.shape[0] // ROUTE_LANES)]
    for h in range(PEER_HEADS):
        oks = [_route_head(h, cols, *refs, break_ties=False) for cols in tiles]
        for cols, ok in zip(tiles, oks):
            @pl.when(jnp.min(ok) < 0.5)
            def _():
                _route_head(h, cols, *refs, break_ties=True)


def _route_head(h, cols, qt_ref, keys_ref, n1_ref, e1_ref, r2_ref, e2_ref, break_ties):
    lanes = ROUTE_LANES
    key_ids = lax.broadcasted_iota(jnp.int32, (N_KEYS, lanes), 0).astype(F32)
    row16 = lax.broadcasted_iota(jnp.int32, (TOPK, lanes), 0).astype(F32)
    row8 = lax.broadcasted_iota(jnp.int32, (8, lanes), 0).astype(F32)
    cand_ids = jnp.concatenate([row16] + [row8 + float(TOPK * p) for p in range(1, 8)]
                               + [(row8 + 8.0) * float(TOPK)], axis=0)
    s1 = jnp.dot(keys_ref[2 * h], qt_ref[(2 * h) * HALF:(2 * h + 1) * HALF, cols].astype(BF16),
                 preferred_element_type=F32)
    s2 = jnp.dot(keys_ref[2 * h + 1], qt_ref[(2 * h + 1) * HALF:(2 * h + 2) * HALF, cols].astype(BF16),
                 preferred_element_type=F32)
    a_vals, rank1 = _extract_top(s1, key_ids, TOPK, break_ties)
    b_vals, rank2 = _extract_top(s2, key_ids, TOPK, break_ties)
    b = jnp.zeros((TOPK, lanes), F32)
    a_hi = jnp.zeros((8, lanes), F32)
    for p in range(TOPK):
        b = jnp.where(row16 == float(p), b_vals[p], b)
    for p in range(8):
        a_hi = jnp.where(row8 == float(p), a_vals[8 + p], a_hi)
    cand = jnp.concatenate([a_vals[0] + b] + [a_vals[p] + b[0:8, :] for p in range(1, 8)]
                           + [a_hi + b_vals[0]], axis=0)
    _, crank = _extract_top(cand, cand_ids, TOPK, break_ties)
    ok = _all_distinct(rank1, TOPK) * _all_distinct(rank2, TOPK) * _all_distinct(crank, TOPK)
    sel = crank < float(TOPK)
    top = a_vals[0] + b_vals[0]
    z = jnp.sum(jnp.where(sel, jnp.exp(cand - top), 0.0), axis=0, keepdims=True)
    self32 = jnp.where(sel, 1.0, 0.0)
    counts = [jnp.sum(self32[0:TOPK, :], axis=0, keepdims=True)]
    counts += [jnp.sum(self32[TOPK + 8 * (p - 1):TOPK + 8 * p, :], axis=0, keepdims=True) for p in range(1, 8)]
    tail = self32[TOPK + 56:TOPK + 64, :]
    counts += [jnp.sum(jnp.where(row8 == float(p), tail, 0.0), axis=0, keepdims=True) for p in range(8)]
    n1 = jnp.zeros((N_KEYS, lanes), F32)
    for p in range(TOPK):
        n1 = jnp.where(rank1 == float(p), counts[p], n1)
    n1_ref[h, :, cols] = n1
    e1_ref[h, :, cols] = jnp.exp(s1 - a_vals[0])
    r2_ref[h, :, cols] = rank2.astype(BF16)
    e2_ref[h, :, cols] = (jnp.exp(s2 - b_vals[0]) * (0.5 / z)).astype(BF16)
    return ok


def _peer_route(h1b, wqt, keys):
    n = h1b.shape[0]
    maps = pl.BlockSpec((PEER_HEADS, N_KEYS, TB_ROUTE), lambda i: (0, 0, i))
    f32_map = jax.ShapeDtypeStruct((PEER_HEADS, N_KEYS, n), F32)
    bf16_map = jax.ShapeDtypeStruct((PEER_HEADS, N_KEYS, n), BF16)
    return pl.pallas_call(
        _peer_route_kernel,
        grid=(n // TB_ROUTE,),
        in_specs=[pl.BlockSpec((TB_ROUTE, D_MODEL), lambda i: (i, 0)),
                  pl.BlockSpec((2 * PEER_HEADS * HALF, D_MODEL), lambda i: (0, 0)),
                  pl.BlockSpec((2 * PEER_HEADS, N_KEYS, HALF), lambda i: (0, 0, 0))],
        out_specs=[maps] * 4,
        out_shape=[f32_map, f32_map, bf16_map, bf16_map],
        scratch_shapes=[pltpu.VMEM((2 * PEER_HEADS * HALF, TB_ROUTE), F32)],
        compiler_params=_params(("parallel",)),
        name="peer_route",
    )(h1b, wqt, keys)


SLAB = 16


def _peer_dense_kernel(xb_ref, x_ref, u_ref, vt_ref, n1_ref, e1_ref, r2_ref, e2_ref, lg_ref, lb_ref,
                       y_ref, ytail_ref, acc_ref, ht_ref, act_ref):
    c = pl.program_id(1)

    @pl.when(c == 0)
    def _():
        acc_ref[...] = jnp.zeros_like(acc_ref)

    lanes = ht_ref.shape[1]
    zero = jnp.zeros((SLAB, lanes), BF16)
    xb = xb_ref[...]
    for sub in range(I_PER_CHUNK // I_PER_SUB):
        sub_rows = slice(sub * E_SUB, (sub + 1) * E_SUB)
        ht_ref[sub_rows, :] = lax.dot_general(u_ref[sub_rows, :], xb, (((1,), (1,)), ((), ())),
                                              preferred_element_type=F32)
    for sub in range(I_PER_CHUNK // I_PER_SUB):
        sub_rows = slice(sub * E_SUB, (sub + 1) * E_SUB)
        for ii in range(sub * I_PER_SUB, (sub + 1) * I_PER_SUB):
            n1 = [jnp.broadcast_to(n1_ref[h, 0, ii:ii + 1, :], (SLAB, lanes)).astype(BF16)
                  for h in range(PEER_HEADS)]
            e1 = [jnp.broadcast_to(e1_ref[h, 0, ii:ii + 1, :], (SLAB, lanes)).astype(BF16)
                  for h in range(PEER_HEADS)]
            for js in range(N_KEYS // SLAB):
                rows = slice(js * SLAB, (js + 1) * SLAB)
                gate = jnp.where(r2_ref[0, rows, :] < n1[0], e2_ref[0, rows, :], zero) * e1[0]
                for h in range(1, PEER_HEADS):
                    gate = gate + jnp.where(r2_ref[h, rows, :] < n1[h], e2_ref[h, rows, :], zero) * e1[h]
                out_rows = slice(ii * N_KEYS + js * SLAB, ii * N_KEYS + (js + 1) * SLAB)
                ht = ht_ref[out_rows, :]
                gelu2 = ht * (1.0 + lax.erf(ht * SQRT_HALF))
                act_ref[out_rows, :] = gelu2.astype(BF16) * gate
        acc_ref[...] += jnp.dot(vt_ref[:, sub_rows], act_ref[sub_rows, :], preferred_element_type=F32)

    is_tail = pl.program_id(0) == pl.num_programs(0) - 1

    @pl.when(c == pl.num_programs(1) - 1)
    def _():
        y = _layer_norm(ALPHA * x_ref[...] + acc_ref[...].T, lg_ref[...], lb_ref[...])

        @pl.when(jnp.logical_not(is_tail))
        def _():
            y_ref[...] = y

        @pl.when(is_tail)
        def _():
            ytail_ref[...] = y


def _peer_dense(h1b, h1, u, vt, n1, e1, r2, e2, lg, lb):
    n = h1b.shape[0]
    n_blocks = n // TB_PEER
    n_chunks = N_EXPERTS // E_CHUNK
    tok = pl.BlockSpec((TB_PEER, D_MODEL), lambda i, c: (i, 0))
    by_first = pl.BlockSpec((PEER_HEADS, 1, I_PER_CHUNK, TB_PEER), lambda i, c: (0, c, 0, i))
    by_second = pl.BlockSpec((PEER_HEADS, N_KEYS, TB_PEER), lambda i, c: (0, 0, i))
    vec = pl.BlockSpec((1, D_MODEL), lambda i, c: (0, 0))
    return pl.pallas_call(
        _peer_dense_kernel,
        grid=(n // TB_PEER, n_chunks),
        in_specs=[tok, tok,
                  pl.BlockSpec((E_CHUNK, D_MODEL), lambda i, c: (c, 0)),
                  pl.BlockSpec((D_MODEL, E_CHUNK), lambda i, c: (0, c)),
                  by_first, by_first, by_second, by_second, vec, vec],
        out_specs=[pl.BlockSpec((TB_PEER, D_MODEL), lambda i, c: (jnp.minimum(i, n_blocks - 2), 0)),
                   pl.BlockSpec((TB_PEER, D_MODEL), lambda i, c: (0, 0))],
        out_shape=[jax.ShapeDtypeStruct((n - TB_PEER, D_MODEL), F32),
                   jax.ShapeDtypeStruct((TB_PEER, D_MODEL), F32)],
        scratch_shapes=[pltpu.VMEM((D_MODEL, TB_PEER), F32), pltpu.VMEM((E_CHUNK, TB_PEER), F32),
                        pltpu.VMEM((E_CHUNK, TB_PEER), BF16)],
        compiler_params=_params(("arbitrary", "arbitrary")),
        name="peer_dense",
    )(h1b, h1, u, vt, n1, e1, r2, e2, lg, lb)


def _pad_cols(x, width):
    return jnp.pad(x, ((0, 0), (0, width - x.shape[1])))


def _pad_rows(x, height):
    return jnp.pad(x, ((0, height - x.shape[0]), (0, 0)))


def kernel(x_prompt, x_sample, state_conv, state_shift, state_wkv, meta_tokens, ln_in_g, ln_in_b, w_in, conv_w, mu, w0, w_up, a0, a_up, g_up, k_k, k_a, r_k, gn_g, gn_b, w_o, ln1_g, ln1_b, w_q, sub_keys, peer_u, peer_v, ln2_g, ln2_b):
    row = lambda x: x.reshape(1, -1)
    chan = np.arange(RWKV_WIDTH)
    perm = (chan % HEADS) * HEAD + chan // HEADS
    pc = lambda x: x[..., perm]
    w_in0 = w_in[0]
    wcv = w_in0[:, :CV_COLS].astype(BF16)
    rw0 = CV_COLS
    lo0 = rw0 + 3 * RWKV_WIDTH

    def rw_layout(x):
        groups = [pc(x[:, rw0 + i * RWKV_WIDTH:rw0 + (i + 1) * RWKV_WIDTH]) for i in range(3)]
        return jnp.concatenate(groups + [_pad_cols(x[:, lo0:lo0 + W_LORA], LORA_SLOT),
                                         _pad_cols(x[:, lo0 + W_LORA:lo0 + W_LORA + A_LORA], LORA_SLOT),
                                         _pad_cols(x[:, lo0 + W_LORA + A_LORA:], LORA_SLOT)], axis=1)

    wrw = rw_layout(w_in0).astype(BF16)
    mu_p = rw_layout(jnp.pad(mu, ((0, 0), (CV_COLS, 0))))
    ones_hd = jnp.asarray((chan[:, None] % HEADS) == (chan[None, :] % HEADS), BF16)
    lora = lambda w: _pad_rows(pc(w[0]), LORA_SLOT).astype(BF16)
    mix_params = (conv_w[0], mu_p, pc(w0), lora(w_up), pc(a0), lora(a_up), lora(g_up),
                  pc(k_k), pc(k_a), pc(r_k.reshape(1, RWKV_WIDTH)), ones_hd)
    out_params = (pc(gn_g), pc(gn_b), ones_hd, w_o[0, :CONV_WIDTH].astype(BF16),
                  w_o[0, CONV_WIDTH:][perm].astype(BF16), ln1_g, ln1_b)

    dt = x_prompt.dtype
    meta_blk = jnp.concatenate([jnp.zeros((TB_PROJ - N_META, D_MODEL), dt), meta_tokens.astype(dt)], axis=0)
    x_extra = jnp.stack([meta_blk, _pad_rows(x_sample.reshape(DEC_BATCH, D_MODEL), TB_PROJ),
                         jnp.zeros((TB_PROJ, D_MODEL), dt)])
    h_t, pcv_t, prw_t = _ln_proj(x_extra[1:].reshape(2 * TB_PROJ, D_MODEL), row(ln_in_g), row(ln_in_b), wcv, wrw)
    prev_rw_s = _shift_proj(state_shift[0], wrw)

    outs_p = _mixer_seq(x_prompt, x_extra, row(ln_in_g), row(ln_in_b), wcv, wrw,
                        jnp.zeros((BATCH, 8, CONV_WIDTH), F32), jnp.zeros((BATCH, 8, RW_PAD), F32), mix_params)
    scan_p = outs_p[:N_SCAN_IN]
    ycv_p, g_p, bv_p, h = outs_p[N_SCAN_IN:N_SCAN_IN + N_TOK_OUT + 1]
    conv_p = outs_p[N_SCAN_IN + N_TOK_OUT + 1][:, 6:8]
    o_p, st_p = _wkv_seq(scan_p, jnp.zeros((V_ROWS, HEAD, LANES), F32))
    wkv_p = _seq_state_from_scan(st_p)
    h1_p, h1b_p = _mix_out_seq(o_p, g_p, bv_p, ycv_p, h, out_params)

    s_rows = slice(0, DEC_BATCH)
    outs_s = _mixer_rows(pcv_t[s_rows], prw_t[s_rows], prev_rw_s, state_conv[0, :, 1], state_conv[0, :, 0],
                         mix_params)
    r_s, w_s, k_s, nkk_s, bb_s, v_s, ycv_s, g_s, bv_s, z_s = outs_s
    o_s, st_s = _wkv_scan(_to_scan_keys(r_s), _to_scan_keys(w_s), _to_scan_keys(k_s), _to_scan_keys(nkk_s),
                          _to_scan_keys(bb_s), _to_scan_vals(v_s), _state_to_scan(state_wkv[0]))
    wkv_s = _state_from_scan(st_s)
    tail = lambda x: _pad_rows(x, TB_PEER)
    h1, h1b = _mix_out_tail(tail(_from_scan_vals(o_s)), tail(g_s), tail(bv_s), tail(ycv_s), h_t, out_params,
                            h1_p, h1b_p)

    wqt = w_q[0].T.astype(BF16)
    keys = sub_keys[0].reshape(2 * PEER_HEADS, N_KEYS, HALF).astype(BF16)
    n1, e1, r2, e2 = _peer_route(h1b, wqt, keys)
    by_first = lambda x: x.reshape(PEER_HEADS, N_KEYS // I_PER_CHUNK, I_PER_CHUNK, T_PEER)
    y, y_tail = _peer_dense(h1b, h1, peer_u[0].astype(BF16), peer_v[0].T.astype(BF16), by_first(n1), by_first(e1),
                            r2, e2, row(ln2_g), row(ln2_b))

    y_prompt = y.reshape(BATCH, SEQ, D_MODEL)
    y_sample = y_tail[:DEC_BATCH].reshape(DEC_BATCH, 1, D_MODEL)
    shift_p = h[np.arange(BATCH) * S_PITCH + SEQ - 1]
    conv_s = jnp.stack([state_conv[0, :, 1], z_s], axis=1)
    return (y_prompt, y_sample, conv_p[None], shift_p[None], wkv_p[None],
            conv_s[None], h_t[s_rows][None], wkv_s[None])
```

```python
import functools

import numpy as np
import jax
import jax.numpy as jnp
from jax import lax
from jax.experimental import pallas as pl
from jax.experimental.pallas import tpu as pltpu

F32 = jnp.float32
BF16 = jnp.bfloat16

D_MODEL = 1024
N_META = 16
BATCH = 8
SEQ = 2048
DEC_BATCH = 128
T_BLK = 128
S_PITCH = 2304
N_TIME_BLK = SEQ // T_BLK + 1
META_BLK = S_PITCH // T_BLK - 1
T_SEQS = BATCH * S_PITCH
T_PEER = 16896
CONV_WIDTH = 512
RWKV_WIDTH = 512
HEAD = 64
HEADS = 8
W_LORA = 32
A_LORA = 32
G_LORA = 96
CV_COLS = 3 * CONV_WIDTH
LORA_SLOT = 128
RW_PAD = 3 * RWKV_WIDTH + 3 * LORA_SLOT
PEER_HEADS = 8
N_KEYS = 128
N_EXPERTS = N_KEYS * N_KEYS
TOPK = 16
HALF = 128
LN_EPS = 1e-5
GN_EPS = HEAD * 1e-5
ALPHA = 2.0 ** 0.25
SQRT_HALF = float(np.sqrt(0.5))
NOT_RANKED = 256.0
VMEM_LIMIT_BYTES = 56 * 1024 * 1024

TB_PROJ = 256
TB_MIX = 256
TB_OUT = 256
TB_ROUTE = 256
TB_PEER = 512
I_PER_CHUNK = 16
E_CHUNK = I_PER_CHUNK * N_KEYS
I_PER_SUB = 4
E_SUB = I_PER_SUB * N_KEYS


def _params(sem):
    return pltpu.CompilerParams(dimension_semantics=sem, vmem_limit_bytes=VMEM_LIMIT_BYTES)


def _layer_norm(x, g, b):
    m = jnp.mean(x, axis=-1, keepdims=True)
    xc = x - m
    var = jnp.mean(xc * xc, axis=-1, keepdims=True)
    return xc * lax.rsqrt(var + LN_EPS) * g + b


def _headsum(x, ones_hd):
    hi = x.astype(BF16)
    lo = (x - hi.astype(F32)).astype(BF16)
    return (jnp.dot(hi, ones_hd, preferred_element_type=F32)
            + jnp.dot(lo, ones_hd, preferred_element_type=F32))


def _time_block(l):
    return jnp.where(l == 0, META_BLK, l - 1)


S_CHAN = N_TIME_BLK * T_BLK


def _chan_block(l):
    return jnp.where(l == 0, N_TIME_BLK - 1, l - 1)


def _ln_proj_kernel(x_ref, g_ref, b_ref, wcv_ref, wrw_ref, h_ref, pcv_ref, prw_ref):
    h = _layer_norm(x_ref[...], g_ref[...], b_ref[...])
    h_ref[...] = h
    hb = h.astype(BF16)
    pcv_ref[...] = jnp.dot(hb, wcv_ref[...], preferred_element_type=F32)
    prw_ref[...] = jnp.dot(hb, wrw_ref[...], preferred_element_type=F32)


def _ln_proj(x, g, b, wcv, wrw):
    n = x.shape[0]
    row = lambda i: (i, 0)
    fixed = lambda i: (0, 0)
    return pl.pallas_call(
        _ln_proj_kernel,
        grid=(n // TB_PROJ,),
        in_specs=[pl.BlockSpec((TB_PROJ, D_MODEL), row),
                  pl.BlockSpec((1, D_MODEL), fixed),
                  pl.BlockSpec((1, D_MODEL), fixed),
                  pl.BlockSpec((D_MODEL, CV_COLS), fixed),
                  pl.BlockSpec((D_MODEL, RW_PAD), fixed)],
        out_specs=[pl.BlockSpec((TB_PROJ, D_MODEL), row),
                   pl.BlockSpec((TB_PROJ, CV_COLS), row),
                   pl.BlockSpec((TB_PROJ, RW_PAD), row)],
        out_shape=[jax.ShapeDtypeStruct((n, D_MODEL), F32),
                   jax.ShapeDtypeStruct((n, CV_COLS), F32),
                   jax.ShapeDtypeStruct((n, RW_PAD), F32)],
        compiler_params=_params(("parallel",)),
        name="ln_proj",
    )(x, g, b, wcv, wrw)


def _shift_proj_kernel(x_ref, wrw_ref, prw_ref):
    prw_ref[...] = jnp.dot(x_ref[...].astype(BF16), wrw_ref[...], preferred_element_type=F32)


def _shift_proj(x, wrw):
    n = x.shape[0]
    return pl.pallas_call(
        _shift_proj_kernel,
        out_shape=jax.ShapeDtypeStruct((n, RW_PAD), F32),
        compiler_params=_params(None),
        name="shift_proj",
    )(x, wrw)


N_SCAN_IN = 6
N_TOK_OUT = 3


def _mixer_core(pcv, prw, prw_prev, z1, z2, cw, mu, w0, wup, a0, aup, gup, kkw, kaw, rkw, ones_hd):
    bg = pcv[:, 0:CONV_WIDTH]
    z = pcv[:, CONV_WIDTH:2 * CONV_WIDTH] * pcv[:, 2 * CONV_WIDTH:3 * CONV_WIDTH]
    conv = cw[0:1, :] * z2 + cw[1:2, :] * z1 + cw[2:3, :] * z
    y_cv = bg * conv

    m = prw + (prw_prev - prw) * mu
    r = m[:, 0:RWKV_WIDTH]
    k = m[:, RWKV_WIDTH:2 * RWKV_WIDTH]
    v = m[:, 2 * RWKV_WIDTH:3 * RWKV_WIDTH]
    base = 3 * RWKV_WIDTH
    wd = m[:, base:base + LORA_SLOT]
    ad = m[:, base + LORA_SLOT:base + 2 * LORA_SLOT]
    gd = m[:, base + 2 * LORA_SLOT:base + 3 * LORA_SLOT]

    xw = -(w0 + jnp.dot(jnp.tanh(wd).astype(BF16), wup, preferred_element_type=F32))
    softplus = jnp.maximum(xw, 0.0) + jnp.log1p(jnp.exp(-jnp.abs(xw)))
    w_log = -softplus - 0.5
    decay = jnp.exp(-jnp.exp(w_log))
    a = jax.nn.sigmoid(a0 + jnp.dot(ad.astype(BF16), aup, preferred_element_type=F32))
    g = jnp.dot(jax.nn.sigmoid(gd).astype(BF16), gup, preferred_element_type=F32)

    kk = k * kkw
    norm = jnp.sqrt(_headsum(kk * kk, ones_hd))
    kk = kk / jnp.maximum(norm, 1e-12)
    kx = k * (1.0 + (a - 1.0) * kaw)
    bonus_v = _headsum(r * kx * rkw, ones_hd) * v
    return z, (r, decay, kx, -kk, kk * a, v), (y_cv, g, bonus_v)


def _mixer_seq_kernel(xp_ref, xe_ref, lng_ref, lnb_ref, wcv_ref, wrw_ref, conv0_ref, prw0_ref, cw_ref, mu_ref,
                      w0_ref, wup_ref, a0_ref, aup_ref, gup_ref, kkw_ref, kaw_ref, rkw_ref, ones_ref, *rest):
    scan_refs = rest[:N_SCAN_IN]
    tok_refs = rest[N_SCAN_IN:N_SCAN_IN + N_TOK_OUT]
    h_ref, convt_ref = rest[N_SCAN_IN + N_TOK_OUT:N_SCAN_IN + N_TOK_OUT + 2]
    zc_ref, pc_ref = rest[N_SCAN_IN + N_TOK_OUT + 2:]
    l = pl.program_id(1)

    @pl.when(l == 0)
    def _():
        zc_ref[...] = conv0_ref[0]
        pc_ref[...] = prw0_ref[0]

    h = _layer_norm(jnp.where(l == 0, xe_ref[0], xp_ref[0]), lng_ref[...], lnb_ref[...])
    h_ref[...] = h
    hb = h.astype(BF16)
    pcv = jnp.dot(hb, wcv_ref[...], preferred_element_type=F32)
    prw = jnp.dot(hb, wrw_ref[...], preferred_element_type=F32)
    n = pcv.shape[0]
    first = jnp.where(l == 0, n - N_META, 0)
    z = pcv[:, CONV_WIDTH:2 * CONV_WIDTH] * pcv[:, 2 * CONV_WIDTH:3 * CONV_WIDTH]
    row_z = lax.broadcasted_iota(jnp.int32, z.shape, 0)
    row_p = lax.broadcasted_iota(jnp.int32, prw.shape, 0)
    c2 = zc_ref[6:7, :]
    c1 = zc_ref[7:8, :]
    z1 = jnp.where(row_z == first, c1, pltpu.roll(z, 1, axis=0))
    z2 = jnp.where(row_z == first, c2, jnp.where(row_z == first + 1, c1, pltpu.roll(z, 2, axis=0)))
    prw_prev = jnp.where(row_p == first, pc_ref[7:8, :], pltpu.roll(prw, 1, axis=0))

    _, scan_in, tok_out = _mixer_core(pcv, prw, prw_prev, z1, z2, cw_ref[...], mu_ref[...], w0_ref[...],
                                      wup_ref[...], a0_ref[...], aup_ref[...], gup_ref[...], kkw_ref[...],
                                      kaw_ref[...], rkw_ref[...], ones_ref[...])
    for ref, val in zip(scan_refs, scan_in):
        ref[0] = val.T
    for ref, val in zip(tok_refs, tok_out):
        ref[...] = val
    tail = z[n - 8:n, :]
    zc_ref[...] = tail
    pc_ref[...] = prw[n - 8:n, :]
    convt_ref[0] = tail


def _mixer_row_kernel(pcv_ref, prw_ref, prwprev_ref, z1_ref, z2_ref, cw_ref, mu_ref, w0_ref, wup_ref, a0_ref,
                      aup_ref, gup_ref, kkw_ref, kaw_ref, rkw_ref, ones_ref, *rest):
    out_refs = rest[:N_SCAN_IN + N_TOK_OUT]
    z_ref = rest[N_SCAN_IN + N_TOK_OUT]
    z, scan_in, tok_out = _mixer_core(pcv_ref[...], prw_ref[...], prwprev_ref[...], z1_ref[...], z2_ref[...],
                                      cw_ref[...], mu_ref[...], w0_ref[...], wup_ref[...], a0_ref[...],
                                      aup_ref[...], gup_ref[...], kkw_ref[...], kaw_ref[...], rkw_ref[...],
                                      ones_ref[...])
    for ref, val in zip(out_refs, scan_in + tok_out):
        ref[...] = val
    z_ref[...] = z


def _mixer_param_specs():
    fixed = (lambda *_: (0, 0))
    shapes = [(3, CONV_WIDTH), (1, RW_PAD), (1, RWKV_WIDTH), (LORA_SLOT, RWKV_WIDTH), (1, RWKV_WIDTH),
              (LORA_SLOT, RWKV_WIDTH), (LORA_SLOT, RWKV_WIDTH), (1, RWKV_WIDTH), (1, RWKV_WIDTH), (1, RWKV_WIDTH),
              (RWKV_WIDTH, RWKV_WIDTH)]
    return [pl.BlockSpec(s, fixed) for s in shapes]


def _mixer_seq(x_prompt, x_extra, ln_g, ln_b, wcv, wrw, conv0, prw0, mix_params):
    per_seq = S_PITCH // TB_MIX
    phys = lambda l: jnp.where(l == 0, per_seq - 1, l - 1)
    row = lambda b, l: (b * per_seq + phys(l), 0)
    chan = lambda b, l: (b, 0, phys(l))
    seq = lambda b, l: (b, 0, 0)
    fixed2 = lambda b, l: (0, 0)
    return pl.pallas_call(
        _mixer_seq_kernel,
        grid=(BATCH, SEQ // TB_MIX + 1),
        in_specs=[pl.BlockSpec((1, TB_MIX, D_MODEL), lambda b, l: (b, jnp.maximum(l - 1, 0), 0)),
                  pl.BlockSpec((1, TB_MIX, D_MODEL), lambda b, l: (0, 0, 0)),
                  pl.BlockSpec((1, D_MODEL), fixed2), pl.BlockSpec((1, D_MODEL), fixed2),
                  pl.BlockSpec((D_MODEL, CV_COLS), fixed2), pl.BlockSpec((D_MODEL, RW_PAD), fixed2),
                  pl.BlockSpec((1, 8, CONV_WIDTH), seq),
                  pl.BlockSpec((1, 8, RW_PAD), seq)] + _mixer_param_specs(),
        out_specs=[pl.BlockSpec((1, RWKV_WIDTH, TB_MIX), chan)] * N_SCAN_IN
                  + [pl.BlockSpec((TB_MIX, RWKV_WIDTH), row)] * N_TOK_OUT
                  + [pl.BlockSpec((TB_MIX, D_MODEL), row), pl.BlockSpec((1, 8, CONV_WIDTH), seq)],
        out_shape=[jax.ShapeDtypeStruct((BATCH, RWKV_WIDTH, S_PITCH), F32)] * N_SCAN_IN
                  + [jax.ShapeDtypeStruct((T_SEQS, RWKV_WIDTH), F32)] * N_TOK_OUT
                  + [jax.ShapeDtypeStruct((T_SEQS, D_MODEL), F32),
                     jax.ShapeDtypeStruct((BATCH, 8, CONV_WIDTH), F32)],
        scratch_shapes=[pltpu.VMEM((8, CONV_WIDTH), F32), pltpu.VMEM((8, RW_PAD), F32)],
        compiler_params=_params(("arbitrary", "arbitrary")),
        name="mixer_seq",
    )(x_prompt, x_extra, ln_g, ln_b, wcv, wrw, conv0, prw0, *mix_params)


def _mixer_rows(pcv, prw, prw_prev, z1, z2, mix_params):
    n = pcv.shape[0]
    return pl.pallas_call(
        _mixer_row_kernel,
        out_shape=[jax.ShapeDtypeStruct((n, RWKV_WIDTH), F32)] * (N_SCAN_IN + N_TOK_OUT + 1),
        compiler_params=_params(None),
        name="mixer_rows",
    )(pcv, prw, prw_prev, z1, z2, *mix_params)


V_ROWS = HEAD // 2
LANES = 128
PITCH_K = T_BLK + 8
PITCH_O = V_ROWS + 8
RETILE_UNROLL = 4


def _wkv_step(s_scr, r, w, kx, nkk, bb, v_row, o_row):
    for vl in range(V_ROWS):
        sv = s_scr[vl]
        sa = jnp.sum(sv * nkk, axis=0, keepdims=True)
        sn = sv * w + sa * bb + v_row(vl) * kx
        s_scr[vl] = sn
        o_row(vl, jnp.sum(sn * r, axis=0, keepdims=True))


def _wkv_seq_kernel(r_ref, w_ref, k_ref, nkk_ref, bb_ref, v_ref, s0_ref, o_ref, st_ref,
                    s_scr, rs, ws, ks, ns, bs, vs, os_):
    l = pl.program_id(0)

    @pl.when(l == 0)
    def _():
        s_scr[...] = s0_ref[...]
        os_[...] = jnp.zeros_like(os_)

    def retile(src_ref, dst_ref, idx, second_idx):
        lo = pl.multiple_of(idx * 8, 8)
        hi = pl.multiple_of(second_idx * 8, 8)
        tile = jnp.concatenate([x for b in range(BATCH)
                                for x in (src_ref[b, pl.ds(lo, 8), :], src_ref[b, pl.ds(hi, 8), :])], axis=0)
        dst_ref[pl.ds(pl.multiple_of(idx * PITCH_K, 8), T_BLK), :] = tile.T

    def retile_keys(n, carry):
        for src, dst in ((r_ref, rs), (w_ref, ws), (k_ref, ks), (nkk_ref, ns), (bb_ref, bs)):
            retile(src, dst, n, n)
        return carry

    def retile_vals(vl, carry):
        retile(v_ref, vs, vl, vl + V_ROWS)
        return carry

    lax.fori_loop(0, HEAD, retile_keys, 0, unroll=RETILE_UNROLL)
    lax.fori_loop(0, V_ROWS, retile_vals, 0, unroll=RETILE_UNROLL)

    def step(t, carry):
        def keys(ref):
            return jnp.concatenate([ref[pl.ds(g * 8 * PITCH_K + t, 8, stride=PITCH_K), :]
                                    for g in range(HEAD // 8)], axis=0)

        def o_row(vl, val):
            os_[pl.ds(t * PITCH_O + vl, 1), :] = val

        _wkv_step(s_scr, keys(rs), keys(ws), keys(ks), keys(ns), keys(bs),
                  lambda vl: vs[pl.ds(vl * PITCH_K + t, 1), :], o_row)
        return carry

    lax.fori_loop(jnp.where(l == 0, T_BLK - N_META, 0), T_BLK, step, 0)

    def write_out(vl, carry):
        tile = os_[pl.ds(vl, T_BLK, stride=PITCH_O), :].T
        for b in range(BATCH):
            for hf in range(2):
                src = (2 * b + hf) * 8
                o_ref[b, pl.ds(pl.multiple_of((hf * V_ROWS + vl) * 8, 8), 8), :] = tile[src:src + 8, :]
        return carry

    lax.fori_loop(0, V_ROWS, write_out, 0, unroll=RETILE_UNROLL)

    @pl.when(l == pl.num_programs(0) - 1)
    def _():
        st_ref[...] = s_scr[...]


def _wkv_seq(scan_in, s0):
    blk = pl.BlockSpec((BATCH, RWKV_WIDTH, T_BLK), lambda l: (0, 0, _chan_block(l)))
    blk_in = pl.BlockSpec((BATCH, RWKV_WIDTH, T_BLK), lambda l: (0, 0, _time_block(l)),
                          pipeline_mode=pl.Buffered(1))
    st = pl.BlockSpec((V_ROWS, HEAD, LANES), lambda l: (0, 0, 0))
    retiled = pltpu.VMEM((HEAD * PITCH_K, LANES), F32)
    return pl.pallas_call(
        _wkv_seq_kernel,
        grid=(N_TIME_BLK,),
        in_specs=[blk_in] * N_SCAN_IN + [st],
        out_specs=[blk, st],
        out_shape=[jax.ShapeDtypeStruct((BATCH, RWKV_WIDTH, S_CHAN), F32),
                   jax.ShapeDtypeStruct((V_ROWS, HEAD, LANES), F32)],
        scratch_shapes=[pltpu.VMEM((V_ROWS, HEAD, LANES), F32)] + [retiled] * 5
                       + [pltpu.VMEM((V_ROWS * PITCH_K, LANES), F32), pltpu.VMEM((T_BLK * PITCH_O, LANES), F32)],
        compiler_params=_params(("arbitrary",)),
        name="wkv_seq",
    )(*scan_in, s0)


def _wkv_kernel(r_ref, w_ref, k_ref, nkk_ref, bb_ref, v_ref, s0_ref, o_ref, st_ref, s_scr):
    s_scr[...] = s0_ref[0]

    def o_row(vl, val):
        o_ref[0, pl.ds(vl, 1), :] = val

    _wkv_step(s_scr, r_ref[0], w_ref[0], k_ref[0], nkk_ref[0], bb_ref[0],
              lambda vl: v_ref[0, pl.ds(vl, 1), :], o_row)
    st_ref[0] = s_scr[...]


def _wkv_scan(r, w, k, nkk, bb, v, s0):
    groups = r.shape[0]
    key_spec = pl.BlockSpec((1, HEAD, LANES), lambda g: (g, 0, 0))
    val_spec = pl.BlockSpec((1, V_ROWS, LANES), lambda g: (g, 0, 0))
    st_spec = pl.BlockSpec((1, V_ROWS, HEAD, LANES), lambda g: (g, 0, 0, 0))
    return pl.pallas_call(
        _wkv_kernel,
        grid=(groups,),
        in_specs=[key_spec] * 5 + [val_spec, st_spec],
        out_specs=[val_spec, st_spec],
        out_shape=[jax.ShapeDtypeStruct((groups, V_ROWS, LANES), F32),
                   jax.ShapeDtypeStruct((groups, V_ROWS, HEAD, LANES), F32)],
        scratch_shapes=[pltpu.VMEM((V_ROWS, HEAD, LANES), F32)],
        compiler_params=_params(("parallel",)),
        name="wkv_scan",
    )(r, w, k, nkk, bb, v, s0)


def _to_scan_keys(x):
    g = x.shape[0] // BATCH
    x = x.reshape(g, BATCH, HEAD, HEADS).transpose(0, 2, 1, 3).reshape(g, HEAD, BATCH * HEADS)
    return jnp.concatenate([x, x], axis=-1)


def _to_scan_vals(x):
    g = x.shape[0] // BATCH
    return x.reshape(g, BATCH, 2, V_ROWS, HEADS).transpose(0, 3, 2, 1, 4).reshape(g, V_ROWS, LANES)


def _from_scan_vals(o):
    g = o.shape[0]
    return o.reshape(g, V_ROWS, 2, BATCH, HEADS).transpose(0, 3, 2, 1, 4).reshape(g * BATCH, RWKV_WIDTH)


def _state_to_scan(s):
    g = s.shape[0] // BATCH
    s = s.reshape(g, BATCH, HEADS, 2, V_ROWS, HEAD).transpose(0, 4, 5, 3, 1, 2)
    return s.reshape(g, V_ROWS, HEAD, LANES)


def _state_from_scan(s):
    g = s.shape[0]
    s = s.reshape(g, V_ROWS, HEAD, 2, BATCH, HEADS).transpose(0, 4, 5, 3, 1, 2)
    return s.reshape(g * BATCH, HEADS, HEAD, HEAD)


def _seq_state_from_scan(s):
    s = s.reshape(V_ROWS, HEAD, BATCH, 2, HEADS).transpose(2, 4, 3, 0, 1)
    return s.reshape(BATCH, HEADS, HEAD, HEAD)


def _mix_out_kernel(o_ref, g_ref, bv_ref, ycv_ref, h_ref, gng_ref, gnb_ref, ones_ref, wo1_ref, wo2_ref,
                    lg_ref, lb_ref, *rest, o_channel_major):
    h1_ref, h1b_ref = rest[-2:]
    ones_hd = ones_ref[...]
    o = o_ref[0].T if o_channel_major else o_ref[...]
    d = o - _headsum(o, ones_hd) * (1.0 / HEAD)
    var = _headsum(d * d, ones_hd) * (1.0 / HEAD)
    gn = d * lax.rsqrt(var + GN_EPS) * gng_ref[...] + gnb_ref[...]
    y_rw = (gn + bv_ref[...]) * g_ref[...]
    mix = (jnp.dot(ycv_ref[...].astype(BF16), wo1_ref[...], preferred_element_type=F32)
           + jnp.dot(y_rw.astype(BF16), wo2_ref[...], preferred_element_type=F32))
    h1 = _layer_norm(ALPHA * h_ref[...] + mix, lg_ref[...], lb_ref[...])
    h1_ref[...] = h1
    h1b_ref[...] = h1.astype(BF16)


def _mix_out_call(kernel, grid, o_spec, tok_map, out_map, n_out, tb, args):
    fixed = lambda *_: (0, 0)
    tok512 = pl.BlockSpec((tb, RWKV_WIDTH), tok_map)
    return pl.pallas_call(
        kernel,
        grid=grid,
        in_specs=[o_spec, tok512, tok512, tok512, pl.BlockSpec((tb, D_MODEL), tok_map),
                  pl.BlockSpec((1, RWKV_WIDTH), fixed), pl.BlockSpec((1, RWKV_WIDTH), fixed),
                  pl.BlockSpec((RWKV_WIDTH, RWKV_WIDTH), fixed),
                  pl.BlockSpec((CONV_WIDTH, D_MODEL), fixed), pl.BlockSpec((RWKV_WIDTH, D_MODEL), fixed),
                  pl.BlockSpec((1, D_MODEL), fixed), pl.BlockSpec((1, D_MODEL), fixed)],
        out_specs=[pl.BlockSpec((tb, D_MODEL), out_map), pl.BlockSpec((tb, D_MODEL), out_map)],
        out_shape=[jax.ShapeDtypeStruct((n_out, D_MODEL), F32), jax.ShapeDtypeStruct((n_out, D_MODEL), BF16)],
        compiler_params=_params(("parallel",) * len(grid)),
        name="mix_out",
    )(*args)


def _mix_out_seq(o, g, bv, ycv, h, out_params):
    per_seq = S_PITCH // TB_OUT
    real = SEQ // TB_OUT
    return _mix_out_call(functools.partial(_mix_out_kernel, o_channel_major=True), (BATCH, real),
                         pl.BlockSpec((1, RWKV_WIDTH, TB_OUT), lambda b, j: (b, 0, j)),
                         lambda b, j: (b * per_seq + j, 0), lambda b, j: (b * real + j, 0),
                         T_PEER, TB_OUT, (o, g, bv, ycv, h, *out_params))


def _mix_out_tail(o, g, bv, ycv, h, out_params, h1, h1b):
    tok = lambda i: (0, 0)
    fixed = lambda *_: (0, 0)
    last = lambda i: (T_PEER // TB_PEER - 1, 0)
    tok512 = pl.BlockSpec((TB_PEER, RWKV_WIDTH), tok)
    whole = pl.BlockSpec(memory_space=pl.ANY)
    n_in = 5 + len(out_params)
    return pl.pallas_call(
        functools.partial(_mix_out_kernel, o_channel_major=False),
        grid=(1,),
        in_specs=[tok512, tok512, tok512, tok512, pl.BlockSpec((TB_PEER, D_MODEL), tok),
                  pl.BlockSpec((1, RWKV_WIDTH), fixed), pl.BlockSpec((1, RWKV_WIDTH), fixed),
                  pl.BlockSpec((RWKV_WIDTH, RWKV_WIDTH), fixed),
                  pl.BlockSpec((CONV_WIDTH, D_MODEL), fixed), pl.BlockSpec((RWKV_WIDTH, D_MODEL), fixed),
                  pl.BlockSpec((1, D_MODEL), fixed), pl.BlockSpec((1, D_MODEL), fixed), whole, whole],
        out_specs=[pl.BlockSpec((TB_PEER, D_MODEL), last), pl.BlockSpec((TB_PEER, D_MODEL), last)],
        out_shape=[jax.ShapeDtypeStruct(h1.shape, F32), jax.ShapeDtypeStruct(h1b.shape, BF16)],
        input_output_aliases={n_in: 0, n_in + 1: 1},
        compiler_params=_params(("arbitrary",)),
        name="mix_out_tail",
    )(o, g, bv, ycv, h, *out_params, h1, h1b)


CAND_ROWS = TOPK + 7 * 8 + 8
ROUTE_LANES = TB_ROUTE


def _extract_top(s, ids, count, break_ties):
    rank = jnp.full(s.shape, NOT_RANKED, F32)
    big = jnp.float32(NOT_RANKED)
    vals = []
    for p in range(count):
        m = jnp.max(s, axis=0, keepdims=True)
        if break_ties:
            hit = ids == jnp.min(jnp.where(s == m, ids, big), axis=0, keepdims=True)
        else:
            hit = s == m
        rank = jnp.where(hit, jnp.float32(p), rank)
        s = jnp.where(hit, -jnp.inf, s)
        vals.append(m)
    return vals, rank


def _all_distinct(rank, count):
    picked = jnp.sum(jnp.where(rank < float(count), 1.0, 0.0), axis=0, keepdims=True)
    return jnp.where(picked == float(count), 1.0, 0.0)


def _peer_route_kernel(x_ref, wqt_ref, keys_ref, n1_ref, e1_ref, r2_ref, e2_ref, qt_ref):
    qt_ref[...] = lax.dot_general(wqt_ref[...], x_ref[...], (((1,), (1,)), ((), ())),
                                  preferred_element_type=F32)
    refs = (qt_ref, keys_ref, n1_ref, e1_ref, r2_ref, e2_ref)
    tiles = [slice(t * ROUTE_LANES, (t + 1) * ROUTE_LANES) for t in range(x_ref.shape[0] // ROUTE_LANES)]
    for h in range(PEER_HEADS):
        oks = [_route_head(h, cols, *refs, break_ties=False) for cols in tiles]
        for cols, ok in zip(tiles, oks):
            @pl.when(jnp.min(ok) < 0.5)
            def _():
                _route_head(h, cols, *refs, break_ties=True)


def _route_head(h, cols, qt_ref, keys_ref, n1_ref, e1_ref, r2_ref, e2_ref, break_ties):
    lanes = ROUTE_LANES
    key_ids = lax.broadcasted_iota(jnp.int32, (N_KEYS, lanes), 0).astype(F32)
    row16 = lax.broadcasted_iota(jnp.int32, (TOPK, lanes), 0).astype(F32)
    row8 = lax.broadcasted_iota(jnp.int32, (8, lanes), 0).astype(F32)
    cand_ids = jnp.concatenate([row16] + [row8 + float(TOPK * p) for p in range(1, 8)]
                               + [(row8 + 8.0) * float(TOPK)], axis=0)
    s1 = jnp.dot(keys_ref[2 * h], qt_ref[(2 * h) * HALF:(2 * h + 1) * HALF, cols].astype(BF16),
                 preferred_element_type=F32)
    s2 = jnp.dot(keys_ref[2 * h + 1], qt_ref[(2 * h + 1) * HALF:(2 * h + 2) * HALF, cols].astype(BF16),
                 preferred_element_type=F32)
    a_vals, rank1 = _extract_top(s1, key_ids, TOPK, break_ties)
    b_vals, rank2 = _extract_top(s2, key_ids, TOPK, break_ties)
    b = jnp.zeros((TOPK, lanes), F32)
    a_hi = jnp.zeros((8, lanes), F32)
    for p in range(TOPK):
        b = jnp.where(row16 == float(p), b_vals[p], b)
    for p in range(8):
        a_hi = jnp.where(row8 == float(p), a_vals[8 + p], a_hi)
    cand = jnp.concatenate([a_vals[0] + b] + [a_vals[p] + b[0:8, :] for p in range(1, 8)]
                           + [a_hi + b_vals[0]], axis=0)
    _, crank = _extract_top(cand, cand_ids, TOPK, break_ties)
    ok = _all_distinct(rank1, TOPK) * _all_distinct(rank2, TOPK) * _all_distinct(crank, TOPK)
    sel = crank < float(TOPK)
    top = a_vals[0] + b_vals[0]
    z = jnp.sum(jnp.where(sel, jnp.exp(cand - top), 0.0), axis=0, keepdims=True)
    self32 = jnp.where(sel, 1.0, 0.0)
    counts = [jnp.sum(self32[0:TOPK, :], axis=0, keepdims=True)]
    counts += [jnp.sum(self32[TOPK + 8 * (p - 1):TOPK + 8 * p, :], axis=0, keepdims=True) for p in range(1, 8)]
    tail = self32[TOPK + 56:TOPK + 64, :]
    counts += [jnp.sum(jnp.where(row8 == float(p), tail, 0.0), axis=0, keepdims=True) for p in range(8)]
    n1 = jnp.zeros((N_KEYS, lanes), F32)
    for p in range(TOPK):
        n1 = jnp.where(rank1 == float(p), counts[p], n1)
    n1_ref[h, :, cols] = n1
    e1_ref[h, :, cols] = jnp.exp(s1 - a_vals[0])
    r2_ref[h, :, cols] = rank2.astype(BF16)
    e2_ref[h, :, cols] = (jnp.exp(s2 - b_vals[0]) * (0.5 / z)).astype(BF16)
    return ok


def _peer_route(h1b, wqt, keys):
    n = h1b.shape[0]
    maps = pl.BlockSpec((PEER_HEADS, N_KEYS, TB_ROUTE), lambda i: (0, 0, i))
    f32_map = jax.ShapeDtypeStruct((PEER_HEADS, N_KEYS, n), F32)
    bf16_map = jax.ShapeDtypeStruct((PEER_HEADS, N_KEYS, n), BF16)
    return pl.pallas_call(
        _peer_route_kernel,
        grid=(n // TB_ROUTE,),
        in_specs=[pl.BlockSpec((TB_ROUTE, D_MODEL), lambda i: (i, 0)),
                  pl.BlockSpec((2 * PEER_HEADS * HALF, D_MODEL), lambda i: (0, 0)),
                  pl.BlockSpec((2 * PEER_HEADS, N_KEYS, HALF), lambda i: (0, 0, 0))],
        out_specs=[maps] * 4,
        out_shape=[f32_map, f32_map, bf16_map, bf16_map],
        scratch_shapes=[pltpu.VMEM((2 * PEER_HEADS * HALF, TB_ROUTE), F32)],
        compiler_params=_params(("parallel",)),
        name="peer_route",
    )(h1b, wqt, keys)


SLAB = 16


def _peer_dense_kernel(xb_ref, x_ref, u_ref, vt_ref, n1_ref, e1_ref, r2_ref, e2_ref, lg_ref, lb_ref,
                       y_ref, ytail_ref, acc_ref, ht_ref, act_ref):
    c = pl.program_id(1)

    @pl.when(c == 0)
    def _():
        acc_ref[...] = jnp.zeros_like(acc_ref)

    lanes = ht_ref.shape[1]
    zero = jnp.zeros((SLAB, lanes), BF16)
    xb = xb_ref[...]
    for sub in range(I_PER_CHUNK // I_PER_SUB):
        sub_rows = slice(sub * E_SUB, (sub + 1) * E_SUB)
        ht_ref[sub_rows, :] = lax.dot_general(u_ref[sub_rows, :], xb, (((1,), (1,)), ((), ())),
                                              preferred_element_type=F32)
    for sub in range(I_PER_CHUNK // I_PER_SUB):
        sub_rows = slice(sub * E_SUB, (sub + 1) * E_SUB)
        for ii in range(sub * I_PER_SUB, (sub + 1) * I_PER_SUB):
            n1 = [jnp.broadcast_to(n1_ref[h, 0, ii:ii + 1, :], (SLAB, lanes)).astype(BF16)
                  for h in range(PEER_HEADS)]
            e1 = [jnp.broadcast_to(e1_ref[h, 0, ii:ii + 1, :], (SLAB, lanes)).astype(BF16)
                  for h in range(PEER_HEADS)]
            for js in range(N_KEYS // SLAB):
                rows = slice(js * SLAB, (js + 1) * SLAB)
                gate = jnp.where(r2_ref[0, rows, :] < n1[0], e2_ref[0, rows, :], zero) * e1[0]
                for h in range(1, PEER_HEADS):
                    gate = gate + jnp.where(r2_ref[h, rows, :] < n1[h], e2_ref[h, rows, :], zero) * e1[h]
                out_rows = slice(ii * N_KEYS + js * SLAB, ii * N_KEYS + (js + 1) * SLAB)
                ht = ht_ref[out_rows, :]
                gelu2 = ht * (1.0 + lax.erf(ht * SQRT_HALF))
                act_ref[out_rows, :] = gelu2.astype(BF16) * gate
        acc_ref[...] += jnp.dot(vt_ref[:, sub_rows], act_ref[sub_rows, :], preferred_element_type=F32)

    is_tail = pl.program_id(0) == pl.num_programs(0) - 1

    @pl.when(c == pl.num_programs(1) - 1)
    def _():
        y = _layer_norm(ALPHA * x_ref[...] + acc_ref[...].T, lg_ref[...], lb_ref[...])

        @pl.when(jnp.logical_not(is_tail))
        def _():
            y_ref[...] = y

        @pl.when(is_tail)
        def _():
            ytail_ref[...] = y


def _peer_dense(h1b, h1, u, vt, n1, e1, r2, e2, lg, lb):
    n = h1b.shape[0]
    n_blocks = n // TB_PEER
    n_chunks = N_EXPERTS // E_CHUNK
    tok = pl.BlockSpec((TB_PEER, D_MODEL), lambda i, c: (i, 0))
    by_first = pl.BlockSpec((PEER_HEADS, 1, I_PER_CHUNK, TB_PEER), lambda i, c: (0, c, 0, i))
    by_second = pl.BlockSpec((PEER_HEADS, N_KEYS, TB_PEER), lambda i, c: (0, 0, i))
    vec = pl.BlockSpec((1, D_MODEL), lambda i, c: (0, 0))
    return pl.pallas_call(
        _peer_dense_kernel,
        grid=(n // TB_PEER, n_chunks),
        in_specs=[tok, tok,
                  pl.BlockSpec((E_CHUNK, D_MODEL), lambda i, c: (c, 0)),
                  pl.BlockSpec((D_MODEL, E_CHUNK), lambda i, c: (0, c)),
                  by_first, by_first, by_second, by_second, vec, vec],
        out_specs=[pl.BlockSpec((TB_PEER, D_MODEL), lambda i, c: (jnp.minimum(i, n_blocks - 2), 0)),
                   pl.BlockSpec((TB_PEER, D_MODEL), lambda i, c: (0, 0))],
        out_shape=[jax.ShapeDtypeStruct((n - TB_PEER, D_MODEL), F32),
                   jax.ShapeDtypeStruct((TB_PEER, D_MODEL), F32)],
        scratch_shapes=[pltpu.VMEM((D_MODEL, TB_PEER), F32), pltpu.VMEM((E_CHUNK, TB_PEER), F32),
                        pltpu.VMEM((E_CHUNK, TB_PEER), BF16)],
        compiler_params=_params(("arbitrary", "arbitrary")),
        name="peer_dense",
    )(h1b, h1, u, vt, n1, e1, r2, e2, lg, lb)


def _pad_cols(x, width):
    return jnp.pad(x, ((0, 0), (0, width - x.shape[1])))


def _pad_rows(x, height):
    return jnp.pad(x, ((0, height - x.shape[0]), (0, 0)))


def kernel(x_prompt, x_sample, state_conv, state_shift, state_wkv, meta_tokens, ln_in_g, ln_in_b, w_in, conv_w, mu, w0, w_up, a0, a_up, g_up, k_k, k_a, r_k, gn_g, gn_b, w_o, ln1_g, ln1_b, w_q, sub_keys, peer_u, peer_v, ln2_g, ln2_b):
    row = lambda x: x.reshape(1, -1)
    chan = np.arange(RWKV_WIDTH)
    perm = (chan % HEADS) * HEAD + chan // HEADS
    pc = lambda x: x[..., perm]
    w_in0 = w_in[0]
    wcv = w_in0[:, :CV_COLS].astype(BF16)
    rw0 = CV_COLS
    lo0 = rw0 + 3 * RWKV_WIDTH

    def rw_layout(x):
        groups = [pc(x[:, rw0 + i * RWKV_WIDTH:rw0 + (i + 1) * RWKV_WIDTH]) for i in range(3)]
        return jnp.concatenate(groups + [_pad_cols(x[:, lo0:lo0 + W_LORA], LORA_SLOT),
                                         _pad_cols(x[:, lo0 + W_LORA:lo0 + W_LORA + A_LORA], LORA_SLOT),
                                         _pad_cols(x[:, lo0 + W_LORA + A_LORA:], LORA_SLOT)], axis=1)

    wrw = rw_layout(w_in0).astype(BF16)
    mu_p = rw_layout(jnp.pad(mu, ((0, 0), (CV_COLS, 0))))
    ones_hd = jnp.asarray((chan[:, None] % HEADS) == (chan[None, :] % HEADS), BF16)
    lora = lambda w: _pad_rows(pc(w[0]), LORA_SLOT).astype(BF16)
    mix_params = (conv_w[0], mu_p, pc(w0), lora(w_up), pc(a0), lora(a_up), lora(g_up),
                  pc(k_k), pc(k_a), pc(r_k.reshape(1, RWKV_WIDTH)), ones_hd)
    out_params = (pc(gn_g), pc(gn_b), ones_hd, w_o[0, :CONV_WIDTH].astype(BF16),
                  w_o[0, CONV_WIDTH:][perm].astype(BF16), ln1_g, ln1_b)

    dt = x_prompt.dtype
    meta_blk = jnp.concatenate([jnp.zeros((TB_PROJ - N_META, D_MODEL), dt), meta_tokens.astype(dt)], axis=0)
    x_extra = jnp.stack([meta_blk, _pad_rows(x_sample.reshape(DEC_BATCH, D_MODEL), TB_PROJ),
                         jnp.zeros((TB_PROJ, D_MODEL), dt)])
    h_t, pcv_t, prw_t = _ln_proj(x_extra[1:].reshape(2 * TB_PROJ, D_MODEL), row(ln_in_g), row(ln_in_b), wcv, wrw)
    prev_rw_s = _shift_proj(state_shift[0], wrw)

    outs_p = _mixer_seq(x_prompt, x_extra, row(ln_in_g), row(ln_in_b), wcv, wrw,
                        jnp.zeros((BATCH, 8, CONV_WIDTH), F32), jnp.zeros((BATCH, 8, RW_PAD), F32), mix_params)
    scan_p = outs_p[:N_SCAN_IN]
    ycv_p, g_p, bv_p, h = outs_p[N_SCAN_IN:N_SCAN_IN + N_TOK_OUT + 1]
    conv_p = outs_p[N_SCAN_IN + N_TOK_OUT + 1][:, 6:8]
    o_p, st_p = _wkv_seq(scan_p, jnp.zeros((V_ROWS, HEAD, LANES), F32))
    wkv_p = _seq_state_from_scan(st_p)
    h1_p, h1b_p = _mix_out_seq(o_p, g_p, bv_p, ycv_p, h, out_params)

    s_rows = slice(0, DEC_BATCH)
    outs_s = _mixer_rows(pcv_t[s_rows], prw_t[s_rows], prev_rw_s, state_conv[0, :, 1], state_conv[0, :, 0],
                         mix_params)
    r_s, w_s, k_s, nkk_s, bb_s, v_s, ycv_s, g_s, bv_s, z_s = outs_s
    o_s, st_s = _wkv_scan(_to_scan_keys(r_s), _to_scan_keys(w_s), _to_scan_keys(k_s), _to_scan_keys(nkk_s),
                          _to_scan_keys(bb_s), _to_scan_vals(v_s), _state_to_scan(state_wkv[0]))
    wkv_s = _state_from_scan(st_s)
    tail = lambda x: _pad_rows(x, TB_PEER)
    h1, h1b = _mix_out_tail(tail(_from_scan_vals(o_s)), tail(g_s), tail(bv_s), tail(ycv_s), h_t, out_params,
                            h1_p, h1b_p)

    wqt = w_q[0].T.astype(BF16)
    keys = sub_keys[0].reshape(2 * PEER_HEADS, N_KEYS, HALF).astype(BF16)
    n1, e1, r2, e2 = _peer_route(h1b, wqt, keys)
    by_first = lambda x: x.reshape(PEER_HEADS, N_KEYS // I_PER_CHUNK, I_PER_CHUNK, T_PEER)
    y, y_tail = _peer_dense(h1b, h1, peer_u[0].astype(BF16), peer_v[0].T.astype(BF16), by_first(n1), by_first(e1),
                            r2, e2, row(ln2_g), row(ln2_b))

    y_prompt = y.reshape(BATCH, SEQ, D_MODEL)
    y_sample = y_tail[:DEC_BATCH].reshape(DEC_BATCH, 1, D_MODEL)
    shift_p = h[np.arange(BATCH) * S_PITCH + SEQ - 1]
    conv_s = jnp.stack([state_conv[0, :, 1], z_s], axis=1)
    return (y_prompt, y_sample, conv_p[None], shift_p[None], wkv_p[None],
            conv_s[None], h_t[s_rows][None], wkv_s[None])
```
